```python
import jax, jax.numpy as jnp
from jax import lax
import numpy as np

D_MODEL = 4096
BATCH = 2
SEQ = 4096
DEPTH = 2

CHUNK = 64
N_MIXERS = 2
D_FF = 11008
CONV_WIDTH = 3
RET_HEADS = 16
RET_QK_DIM = D_MODEL // RET_HEADS
RET_V_DIM = 2 * D_MODEL // RET_HEADS
ROPE_BASE = 10000.0
EPS = 1e-6
GN_EPS = 1e-5

kernel_name = "hybrid_shortconv_retention_macaron"


def rmsnorm(x, g):
    xf = x.astype(jnp.float32)
    y = xf * lax.rsqrt(jnp.mean(xf * xf, axis=-1, keepdims=True) + EPS)
    return (y * g.astype(jnp.float32)).astype(x.dtype)


def swiglu_ffn(x, w_in, w_out):
    gate, up = jnp.split(x @ w_in, 2, axis=-1)
    return (jax.nn.silu(gate) * up) @ w_out


def short_conv_mixer(x, w_in, conv_w, w_out):
    seq = x.shape[1]
    b, c, h = jnp.split(x @ w_in, 3, axis=-1)
    u = jnp.pad(c * h, ((0, 0), (CONV_WIDTH - 1, 0), (0, 0)))
    conv = sum(conv_w[k] * u[:, k:k + seq] for k in range(CONV_WIDTH))
    return (b * conv) @ w_out


def rotary(t, pos):
    half = t.shape[-1] // 2
    inv = ROPE_BASE ** (-jnp.arange(half, dtype=jnp.float32) / half)
    ang = pos.astype(jnp.float32)[:, None] * inv[None, :]
    cos = jnp.cos(ang)[None, :, None, :]
    sin = jnp.sin(ang)[None, :, None, :]
    tf = t.astype(jnp.float32)
    t1, t2 = tf[..., :half], tf[..., half:]
    return jnp.concatenate([t1 * cos - t2 * sin, t1 * sin + t2 * cos], axis=-1).astype(t.dtype)


def retention_mixer(x, w_in, gn_g, w_out):
    bsz, seq, _ = x.shape
    nc = seq // CHUNK
    dt = x.dtype
    q, k, v, g = jnp.split(x @ w_in, [D_MODEL, 2 * D_MODEL, 4 * D_MODEL], axis=-1)
    pos = jnp.arange(seq, dtype=jnp.int32)
    q = rotary(q.reshape(bsz, seq, RET_HEADS, RET_QK_DIM), pos) * RET_QK_DIM ** -0.5
    k = rotary(k.reshape(bsz, seq, RET_HEADS, RET_QK_DIM), pos)
    v = v.reshape(bsz, seq, RET_HEADS, RET_V_DIM)

    log_gamma = jnp.log1p(-jnp.exp2(-5.0 - jnp.arange(RET_HEADS, dtype=jnp.float32)))
    idx = jnp.arange(CHUNK, dtype=jnp.float32)
    dist = jnp.abs(idx[:, None] - idx[None, :])
    decay_intra = jnp.exp(log_gamma[:, None, None] * dist).astype(dt)
    q_dec = jnp.exp(idx[:, None] * log_gamma[None, :] + log_gamma[None, :]).astype(dt)
    k_dec = jnp.exp((CHUNK - 1 - idx)[:, None] * log_gamma[None, :]).astype(dt)
    chunk_dec = jnp.exp(CHUNK * log_gamma).astype(dt)

    qc = q.reshape(bsz, nc, CHUNK, RET_HEADS, RET_QK_DIM)
    kc = k.reshape(bsz, nc, CHUNK, RET_HEADS, RET_QK_DIM)
    vc = v.reshape(bsz, nc, CHUNK, RET_HEADS, RET_V_DIM)

    s = jnp.einsum('bnihd,bnjhd->bnhij', qc, kc) * decay_intra[None, None]
    o_intra = jnp.einsum('bnhij,bnjhe->bnihe', s, vc)

    q_x = jnp.moveaxis(qc * q_dec[None, None, :, :, None], 1, 0)
    k_x = jnp.moveaxis(kc * k_dec[None, None, :, :, None], 1, 0)
    v_x = jnp.moveaxis(vc, 1, 0)

    def step(state, inp):
        qi, ki, vi = inp
        o = jnp.einsum('bihd,bhde->bihe', qi, state)
        state = chunk_dec[None, :, None, None] * state + jnp.einsum('bjhd,bjhe->bhde', ki, vi)
        return state, o

    state0 = jnp.zeros((bsz, RET_HEADS, RET_QK_DIM, RET_V_DIM), dtype=dt)
    _, o_cross = lax.scan(step, state0, (q_x, k_x, v_x))
    o = (o_intra + jnp.moveaxis(o_cross, 0, 1)).reshape(bsz, seq, RET_HEADS, RET_V_DIM)

    of = o.astype(jnp.float32)
    mu = jnp.mean(of, axis=-1, keepdims=True)
    var = jnp.mean(jnp.square(of - mu), axis=-1, keepdims=True)
    y = ((of - mu) * lax.rsqrt(var + GN_EPS)).reshape(bsz, seq, 2 * D_MODEL)
    y = (y * gn_g.astype(jnp.float32)).astype(dt)
    return (jax.nn.silu(g) * y) @ w_out


def setup_inputs(seed: int = 0) -> dict:
    key = jax.random.key(seed)
    ks = iter(jax.random.split(key, 32))

    def w(shape, fan_in):
        return jax.random.normal(next(ks), shape, dtype=jnp.float32) * fan_in ** -0.5

    def gain(n):
        return 1.0 + 0.01 * jax.random.normal(next(ks), (n,), dtype=jnp.float32)

    d = D_MODEL
    inp = {}
    inp["x"] = jax.random.normal(next(ks), (BATCH, SEQ, d), dtype=jnp.float32)
    inp["l0_norm_ffn1"] = gain(d)
    inp["l0_ffn1_w_in"] = w((d, 2 * D_FF), d)
    inp["l0_ffn1_w_out"] = w((D_FF, d), D_FF)
    inp["l0_norm_mix"] = gain(d)
    inp["l0_conv_w_in"] = w((d, 3 * d), d)
    inp["l0_conv_w"] = w((CONV_WIDTH, d), CONV_WIDTH)
    inp["l0_conv_w_out"] = w((d, d), d)
    inp["l0_norm_ffn2"] = gain(d)
    inp["l0_ffn2_w_in"] = w((d, 2 * D_FF), d)
    inp["l0_ffn2_w_out"] = w((D_FF, d), D_FF)
    inp["l1_norm_ffn1"] = gain(d)
    inp["l1_ffn1_w_in"] = w((d, 2 * D_FF), d)
    inp["l1_ffn1_w_out"] = w((D_FF, d), D_FF)
    inp["l1_norm_mix"] = gain(d)
    inp["l1_ret_w_in"] = w((d, 6 * d), d)
    inp["l1_ret_gn"] = gain(2 * d)
    inp["l1_ret_w_out"] = w((2 * d, d), 2 * d)
    inp["l1_norm_ffn2"] = gain(d)
    inp["l1_ffn2_w_in"] = w((d, 2 * D_FF), d)
    inp["l1_ffn2_w_out"] = w((D_FF, d), D_FF)
    inp["final_norm"] = gain(d)
    return inp


def reference(x,
              l0_norm_ffn1, l0_ffn1_w_in, l0_ffn1_w_out,
              l0_norm_mix, l0_conv_w_in, l0_conv_w, l0_conv_w_out,
              l0_norm_ffn2, l0_ffn2_w_in, l0_ffn2_w_out,
              l1_norm_ffn1, l1_ffn1_w_in, l1_ffn1_w_out,
              l1_norm_mix, l1_ret_w_in, l1_ret_gn, l1_ret_w_out,
              l1_norm_ffn2, l1_ffn2_w_in, l1_ffn2_w_out,
              final_norm):
    layers = [
        ((l0_norm_ffn1, l0_ffn1_w_in, l0_ffn1_w_out),
         (l0_norm_mix, (l0_conv_w_in, l0_conv_w, l0_conv_w_out)),
         (l0_norm_ffn2, l0_ffn2_w_in, l0_ffn2_w_out)),
        ((l1_norm_ffn1, l1_ffn1_w_in, l1_ffn1_w_out),
         (l1_norm_mix, (l1_ret_w_in, l1_ret_gn, l1_ret_w_out)),
         (l1_norm_ffn2, l1_ffn2_w_in, l1_ffn2_w_out)),
    ]
    mixers = (short_conv_mixer, retention_mixer)
    h = x
    for i in range(DEPTH):
        (g1, f1_in, f1_out), (gm, mix_params), (g2, f2_in, f2_out) = layers[i]
        h = h + 0.5 * swiglu_ffn(rmsnorm(h, g1), f1_in, f1_out)
        h = h + mixers[i % N_MIXERS](rmsnorm(h, gm), *mix_params)
        h = h + 0.5 * swiglu_ffn(rmsnorm(h, g2), f2_in, f2_out)
    return rmsnorm(h, final_norm)
```

```python
import functools

import jax
import jax.numpy as jnp
from jax import lax
from jax.experimental import pallas as pl
from jax.experimental.pallas import tpu as pltpu

D_MODEL = 4096
D_FF = 11008
CHUNK = 64
CONV_WIDTH = 3
RET_HEADS = 16
RET_QK_DIM = D_MODEL // RET_HEADS
RET_V_DIM = 2 * D_MODEL // RET_HEADS
ROPE_BASE = 10000.0
EPS = 1e-6
GN_EPS = 1e-5

F32 = jnp.float32
BF16 = jnp.bfloat16

V7X_VMEM_LIMIT_BYTES = 60000 * 1024
LANES = 128


def _params(vmem_bytes):
    assert vmem_bytes <= V7X_VMEM_LIMIT_BYTES, vmem_bytes
    return pltpu.CompilerParams(
        dimension_semantics=("arbitrary", "arbitrary"),
        vmem_limit_bytes=V7X_VMEM_LIMIT_BYTES,
    )


NORM_ROWS = 16


def _rms(x, g):
    ms = jnp.mean(x * x, axis=-1, keepdims=True)
    return (x * lax.rsqrt(ms + EPS)) * g


def _rms_rows(src_ref, g_ref, dst_ref, copy_ref=None):
    g = g_ref[...]

    def step(r, carry):
        rows = pl.ds(pl.multiple_of(r * NORM_ROWS, NORM_ROWS), NORM_ROWS)
        x = src_ref[rows, :]
        dst_ref[rows, :] = _rms(x, g).astype(dst_ref.dtype)
        if copy_ref is not None:
            copy_ref[rows, :] = x
        return carry

    lax.fori_loop(0, src_ref.shape[0] // NORM_ROWS, step, 0)


def _silu(x):
    return x * jax.nn.sigmoid(x)


def _ffn_body(h_ref, g_ref, wg_ref, wu_ref, wo_ref, fg_ref, out_ref, xn_ref, *, n_j, final):
    j = pl.program_id(1)

    @pl.when(j == 0)
    def _():
        _rms_rows(h_ref, g_ref, xn_ref, copy_ref=out_ref)

    xn = xn_ref[...]
    gate = jnp.dot(xn, wg_ref[...], preferred_element_type=F32)
    up = jnp.dot(xn, wu_ref[...], preferred_element_type=F32)
    act = (0.5 * _silu(gate) * up).astype(BF16)
    out_ref[...] += jnp.dot(act, wo_ref[...], preferred_element_type=F32)

    if final:
        @pl.when(j == n_j - 1)
        def _():
            _rms_rows(out_ref, fg_ref, out_ref)


def _ffn(h, g, w_in, w_out, final_g=None, *, tm=512, tf=256):
    m, d = h.shape
    d_ff = w_out.shape[0]
    n_j = d_ff // tf
    assert d_ff % tf == 0 and m % tm == 0
    final = final_g is not None
    fg = final_g if final else g
    vmem = 2 * tm * d * 4 * 2 + tm * d * 2 + 2 * 3 * d * tf * 2
    return pl.pallas_call(
        functools.partial(_ffn_body, n_j=n_j, final=final),
        grid=(m // tm, n_j),
        in_specs=[
            pl.BlockSpec((tm, d), lambda i, j: (i, 0)),
            pl.BlockSpec((1, d), lambda i, j: (0, 0)),
            pl.BlockSpec((d, tf), lambda i, j: (0, j)),
            pl.BlockSpec((d, tf), lambda i, j: (0, j + n_j)),
            pl.BlockSpec((tf, d), lambda i, j: (j, 0)),
            pl.BlockSpec((1, d), lambda i, j: (0, 0)),
        ],
        out_specs=pl.BlockSpec((tm, d), lambda i, j: (i, 0)),
        out_shape=jax.ShapeDtypeStruct((m, d), F32),
        scratch_shapes=[pltpu.VMEM((tm, d), BF16)],
        compiler_params=_params(vmem),
        name="ffn",
    )(h, g.reshape(1, d), w_in, w_in, w_out, fg.reshape(1, d))


def _conv_body(h_ref, g_ref, wb_ref, wc_ref, wh_ref, cw_ref, wo_ref, out_ref,
               xn_ref, carry_ref, *, tiles_per_seq):
    i = pl.program_id(0)
    j = pl.program_id(1)

    @pl.when(j == 0)
    def _():
        _rms_rows(h_ref, g_ref, xn_ref, copy_ref=out_ref)

    xn = xn_ref[...]
    b = jnp.dot(xn, wb_ref[...], preferred_element_type=F32)
    c = jnp.dot(xn, wc_ref[...], preferred_element_type=F32)
    hh = jnp.dot(xn, wh_ref[...], preferred_element_type=F32)
    u = c * hh
    tm = u.shape[0]
    @pl.when(i % tiles_per_seq == 0)
    def _():
        carry_ref[j] = jnp.zeros(carry_ref.shape[1:], F32)

    prev = carry_ref[j]
    carry_ref[j] = u[tm - 8:, :]
    row = lax.broadcasted_iota(jnp.int32, u.shape, 0)
    u1 = jnp.where(row < 1, prev[7:8, :], pltpu.roll(u, 1, 0))
    u2 = jnp.where(row < 2, jnp.where(row < 1, prev[6:7, :], prev[7:8, :]), pltpu.roll(u, 2, 0))
    cw = cw_ref[...]
    conv = cw[0:1, :] * u2 + cw[1:2, :] * u1 + cw[2:3, :] * u
    y = (b * conv).astype(BF16)
    out_ref[...] += jnp.dot(y, wo_ref[...], preferred_element_type=F32)


def _conv_mixer(h, g, w_in, conv_w, w_out, *, seq, tm=512, tc=256):
    m, d = h.shape
    n_j = d // tc
    assert m % tm == 0 and seq % tm == 0 and d % tc == 0
    vmem = 2 * tm * d * 4 * 2 + tm * d * 2 + 2 * 4 * d * tc * 2 + n_j * 8 * tc * 4
    return pl.pallas_call(
        functools.partial(_conv_body, tiles_per_seq=seq // tm),
        grid=(m // tm, n_j),
        in_specs=[
            pl.BlockSpec((tm, d), lambda i, j: (i, 0), pipeline_mode=pl.Buffered(1)),
            pl.BlockSpec((1, d), lambda i, j: (0, 0)),
            pl.BlockSpec((d, tc), lambda i, j: (0, j)),
            pl.BlockSpec((d, tc), lambda i, j: (0, j + n_j)),
            pl.BlockSpec((d, tc), lambda i, j: (0, j + 2 * n_j)),
            pl.BlockSpec((CONV_WIDTH, tc), lambda i, j: (0, j)),
            pl.BlockSpec((tc, d), lambda i, j: (j, 0)),
        ],
        out_specs=pl.BlockSpec((tm, d), lambda i, j: (i, 0)),
        out_shape=jax.ShapeDtypeStruct((m, d), F32),
        scratch_shapes=[pltpu.VMEM((tm, d), BF16), pltpu.VMEM((n_j, 8, tc), F32)],
        compiler_params=_params(vmem),
        name="conv_mixer",
    )(h, g.reshape(1, d), w_in, w_in, w_in, conv_w, w_out)


def _proj_body(h_ref, g_ref, w_ref, *rest, n_q_tiles):
    *rope_refs, out_ref, xn_ref = rest
    j = pl.program_id(1)

    @pl.when(j == 0)
    def _():
        _rms_rows(h_ref, g_ref, xn_ref)

    t = jnp.dot(xn_ref[...], w_ref[...], preferred_element_type=F32)
    if not rope_refs:
        out_ref[...] = t.astype(BF16)
        return
    cos_ref, sin_ref = rope_refs
    cos = cos_ref[...]
    sin = sin_ref[...]
    scale = jnp.where(j < n_q_tiles, RET_QK_DIM ** -0.5, 1.0)
    half = RET_QK_DIM // 2
    for hd in range(t.shape[1] // RET_QK_DIM):
        lo = hd * RET_QK_DIM
        t1 = t[:, lo:lo + half]
        t2 = t[:, lo + half:lo + 2 * half]
        out_ref[:, lo:lo + half] = ((t1 * cos - t2 * sin) * scale).astype(BF16)
        out_ref[:, lo + half:lo + 2 * half] = ((t1 * sin + t2 * cos) * scale).astype(BF16)


def _proj(h, g, w, rope_tables=(), *, col0, n, seq, tm=1024, tn=512):
    m, d = h.shape
    assert m % tm == 0 and n % tn == 0 and col0 % tn == 0 and seq % tm == 0
    half = RET_QK_DIM // 2
    assert half == LANES and tn % RET_QK_DIM == 0
    j0 = col0 // tn
    tiles_per_seq = seq // tm
    vmem = 2 * tm * d * 4 + tm * d * 2 + 2 * d * tn * 2 + 2 * tm * tn * 2 + 4 * tm * half * 4
    table_spec = pl.BlockSpec((tm, half), lambda i, j: (i % tiles_per_seq, 0))
    return pl.pallas_call(
        functools.partial(_proj_body, n_q_tiles=D_MODEL // tn),
        grid=(m // tm, n // tn),
        in_specs=[
            pl.BlockSpec((tm, d), lambda i, j: (i, 0), pipeline_mode=pl.Buffered(1)),
            pl.BlockSpec((1, d), lambda i, j: (0, 0)),
            pl.BlockSpec((d, tn), lambda i, j: (0, j + j0)),
        ] + [table_spec] * len(rope_tables),
        out_specs=pl.BlockSpec((tm, tn), lambda i, j: (i, j)),
        out_shape=jax.ShapeDtypeStruct((m, n), BF16),
        scratch_shapes=[pltpu.VMEM((tm, d), BF16)],
        compiler_params=_params(vmem),
        name="ret_proj_rope" if rope_tables else "ret_proj",
    )(h, g.reshape(1, d), w, *rope_tables)


def _ret_body(lg_ref, q_ref, k_ref, v_ref, gate_ref, gn_ref, out_ref, state_ref, *, t_blk):
    hd = pl.program_id(1)
    t = pl.program_id(2)
    lg = lg_ref[hd]

    @pl.when(t == 0)
    def _():
        state_ref[...] = jnp.zeros_like(state_ref)

    row = lax.broadcasted_iota(jnp.int32, (t_blk, t_blk), 0)
    col = lax.broadcasted_iota(jnp.int32, (t_blk, t_blk), 1)
    shift = CHUNK.bit_length() - 1
    rc = lax.shift_right_logical(row, shift)
    cc = lax.shift_right_logical(col, shift)
    diff = row - col
    dist = jnp.where(rc == cc, jnp.abs(diff), diff)
    visible = cc <= rc
    dist = jnp.where(visible, dist, 0).astype(F32)
    dmat = jnp.where(visible, jnp.exp(lg * dist), 0.0)

    q = q_ref[...]
    k = k_ref[...]
    v = v_ref[...]
    s = lax.dot_general(q, k, (((1,), (1,)), ((), ())), preferred_element_type=F32)
    p = (s * dmat).astype(BF16)
    o = jnp.dot(p, v, preferred_element_type=F32)

    idx = lax.broadcasted_iota(jnp.int32, (t_blk, 1), 0).astype(F32)
    q_dec = jnp.exp(lg * (idx + 1.0))
    k_dec = jnp.exp(lg * (t_blk - 1.0 - idx))
    qx = (q.astype(F32) * q_dec).astype(BF16)
    kxt = (k.astype(F32) * k_dec).T.astype(BF16)
    st = state_ref[...]
    o = o + jnp.dot(qx, st.astype(BF16), preferred_element_type=F32)
    blk_dec = jnp.exp(jnp.full((1, RET_V_DIM), t_blk, F32) * lg)
    state_ref[...] = blk_dec * st + jnp.dot(kxt, v, preferred_element_type=F32)

    mu = jnp.mean(o, axis=-1, keepdims=True)
    dlt = o - mu
    var = jnp.mean(dlt * dlt, axis=-1, keepdims=True)
    y = (dlt * lax.rsqrt(var + GN_EPS)) * gn_ref[...]
    out_ref[...] = (_silu(gate_ref[...].astype(F32)) * y).astype(BF16)


def _retention(qk, vg, gn_g, log_gamma, *, bsz, seq, t_blk=256):
    m = qk.shape[0]
    nt = seq // t_blk
    assert seq % t_blk == 0 and t_blk % CHUNK == 0
    return pl.pallas_call(
        functools.partial(_ret_body, t_blk=t_blk),
        grid=(bsz, RET_HEADS, nt),
        in_specs=[
            pl.BlockSpec(memory_space=pltpu.SMEM),
            pl.BlockSpec((t_blk, RET_QK_DIM), lambda b, h, t: (b * nt + t, h)),
            pl.BlockSpec((t_blk, RET_QK_DIM), lambda b, h, t: (b * nt + t, h + RET_HEADS)),
            pl.BlockSpec((t_blk, RET_V_DIM), lambda b, h, t: (b * nt + t, h)),
            pl.BlockSpec((t_blk, RET_V_DIM), lambda b, h, t: (b * nt + t, h + RET_HEADS)),
            pl.BlockSpec((1, RET_V_DIM), lambda b, h, t: (0, h)),
        ],
        out_specs=pl.BlockSpec((t_blk, RET_V_DIM), lambda b, h, t: (b * nt + t, h)),
        out_shape=jax.ShapeDtypeStruct((m, RET_HEADS * RET_V_DIM), BF16),
        scratch_shapes=[pltpu.VMEM((RET_QK_DIM, RET_V_DIM), F32)],
        compiler_params=pltpu.CompilerParams(
            dimension_semantics=("arbitrary", "arbitrary", "arbitrary")),
        name="retention",
    )(log_gamma, qk, qk, vg, vg, gn_g.reshape(1, -1))


def _out_proj_body(y_ref, w_ref, h_ref, out_ref):
    out_ref[...] = h_ref[...] + jnp.dot(y_ref[...], w_ref[...], preferred_element_type=F32)


def _out_proj(y, w, h, *, tm=512, tn=512):
    m, kdim = y.shape
    d = w.shape[1]
    assert m % tm == 0 and d % tn == 0
    vmem = 2 * tm * kdim * 2 + 2 * kdim * tn * 2 + 4 * tm * tn * 4
    return pl.pallas_call(
        _out_proj_body,
        grid=(m // tm, d // tn),
        in_specs=[
            pl.BlockSpec((tm, kdim), lambda i, j: (i, 0)),
            pl.BlockSpec((kdim, tn), lambda i, j: (0, j)),
            pl.BlockSpec((tm, tn), lambda i, j: (i, j)),
        ],
        out_specs=pl.BlockSpec((tm, tn), lambda i, j: (i, j)),
        out_shape=jax.ShapeDtypeStruct((m, d), F32),
        compiler_params=_params(vmem),
        name="ret_out_proj",
    )(y, w, h)


def kernel(x, l0_norm_ffn1, l0_ffn1_w_in, l0_ffn1_w_out, l0_norm_mix, l0_conv_w_in, l0_conv_w, l0_conv_w_out, l0_norm_ffn2, l0_ffn2_w_in, l0_ffn2_w_out, l1_norm_ffn1, l1_ffn1_w_in, l1_ffn1_w_out, l1_norm_mix, l1_ret_w_in, l1_ret_gn, l1_ret_w_out, l1_norm_ffn2, l1_ffn2_w_in, l1_ffn2_w_out, final_norm):
    bsz, seq, d = x.shape
    h = x.reshape(bsz * seq, d)
    bf = lambda w: w.astype(BF16)

    h = _ffn(h, l0_norm_ffn1, bf(l0_ffn1_w_in), bf(l0_ffn1_w_out))
    h = _conv_mixer(h, l0_norm_mix, bf(l0_conv_w_in), l0_conv_w, bf(l0_conv_w_out), seq=seq)
    h = _ffn(h, l0_norm_ffn2, bf(l0_ffn2_w_in), bf(l0_ffn2_w_out))

    h = _ffn(h, l1_norm_ffn1, bf(l1_ffn1_w_in), bf(l1_ffn1_w_out))

    half = RET_QK_DIM // 2
    inv = ROPE_BASE ** (-jnp.arange(half, dtype=F32) / half)
    ang = jnp.arange(seq, dtype=jnp.int32).astype(F32)[:, None] * inv[None, :]
    cos, sin = jnp.cos(ang), jnp.sin(ang)
    log_gamma = jnp.log1p(-jnp.exp2(-5.0 - jnp.arange(RET_HEADS, dtype=F32)))

    w_ret = bf(l1_ret_w_in)
    qk = _proj(h, l1_norm_mix, w_ret, (cos, sin), col0=0, n=2 * d, seq=seq)
    vg = _proj(h, l1_norm_mix, w_ret, col0=2 * d, n=4 * d, seq=seq)
    y = _retention(qk, vg, l1_ret_gn, log_gamma, bsz=bsz, seq=seq)
    h = _out_proj(y, bf(l1_ret_w_out), h)

    h = _ffn(h, l1_norm_ffn2, bf(l1_ffn2_w_in), bf(l1_ffn2_w_out), final_norm)
    return h.reshape(bsz, seq, d)
```

```python
import functools

import jax
import jax.numpy as jnp
from jax import lax
from jax.experimental import pallas as pl
from jax.experimental.pallas import tpu as pltpu

D_MODEL = 4096
D_FF = 11008
CHUNK = 64
CONV_WIDTH = 3
RET_HEADS = 16
RET_QK_DIM = D_MODEL // RET_HEADS
RET_V_DIM = 2 * D_MODEL // RET_HEADS
ROPE_BASE = 10000.0
EPS = 1e-6
GN_EPS = 1e-5

F32 = jnp.float32
BF16 = jnp.bfloat16

V7X_VMEM_LIMIT_BYTES = 60000 * 1024
LANES = 128


def _params(vmem_bytes):
    assert vmem_bytes <= V7X_VMEM_LIMIT_BYTES, vmem_bytes
    return pltpu.CompilerParams(
        dimension_semantics=("arbitrary", "arbitrary"),
        vmem_limit_bytes=V7X_VMEM_LIMIT_BYTES,
    )


NORM_ROWS = 16


def _rms(x, g):
    ms = jnp.mean(x * x, axis=-1, keepdims=True)
    return (x * lax.rsqrt(ms + EPS)) * g


def _rms_rows(src_ref, g_ref, dst_ref):
    g = g_ref[...]

    def step(r, carry):
        rows = pl.ds(pl.multiple_of(r * NORM_ROWS, NORM_ROWS), NORM_ROWS)
        dst_ref[rows, :] = _rms(src_ref[rows, :], g).astype(dst_ref.dtype)
        return carry

    lax.fori_loop(0, src_ref.shape[0] // NORM_ROWS, step, 0)


def _silu(x):
    return x * jax.nn.sigmoid(x)


def _row_tile_copy(hbm_ref, vmem_ref, sem, *, to_vmem):
    tm = vmem_ref.shape[0]
    rows = pl.ds(pl.multiple_of(pl.program_id(0) * tm, tm), tm)
    if to_vmem:
        return pltpu.make_async_copy(hbm_ref.at[rows, :], vmem_ref, sem)
    return pltpu.make_async_copy(vmem_ref, hbm_ref.at[rows, :], sem)


def _ffn_body(h_hbm, g_ref, wg_ref, wu_ref, wo_ref, fg_ref, out_hbm, acc_ref, xn_ref, sem,
              *, n_j, final):
    j = pl.program_id(1)

    @pl.when(j == 0)
    def _():
        load = _row_tile_copy(h_hbm, acc_ref, sem.at[0], to_vmem=True)
        load.start()
        load.wait()
        _rms_rows(acc_ref, g_ref, xn_ref)

    xn = xn_ref[...]
    gate = jnp.dot(xn, wg_ref[...], preferred_element_type=F32)
    up = jnp.dot(xn, wu_ref[...], preferred_element_type=F32)
    act = (0.5 * _silu(gate) * up).astype(BF16)
    acc_ref[...] += jnp.dot(act, wo_ref[...], preferred_element_type=F32)

    @pl.when(j == n_j - 1)
    def _():
        if final:
            _rms_rows(acc_ref, fg_ref, acc_ref)
        store = _row_tile_copy(out_hbm, acc_ref, sem.at[1], to_vmem=False)
        store.start()
        store.wait()


def _ffn(h, g, w_in, w_out, final_g=None, *, tm=1024, tf=256):
    m, d = h.shape
    d_ff = w_out.shape[0]
    n_j = d_ff // tf
    assert d_ff % tf == 0 and m % tm == 0
    final = final_g is not None
    fg = final_g if final else g
    vmem = tm * d * 4 + tm * d * 2 + 2 * 3 * d * tf * 2
    return pl.pallas_call(
        functools.partial(_ffn_body, n_j=n_j, final=final),
        grid=(m // tm, n_j),
        in_specs=[
            pl.BlockSpec(memory_space=pl.ANY),
            pl.BlockSpec((1, d), lambda i, j: (0, 0)),
            pl.BlockSpec((d, tf), lambda i, j: (0, j)),
            pl.BlockSpec((d, tf), lambda i, j: (0, j + n_j)),
            pl.BlockSpec((tf, d), lambda i, j: (j, 0)),
            pl.BlockSpec((1, d), lambda i, j: (0, 0)),
        ],
        out_specs=pl.BlockSpec(memory_space=pl.ANY),
        out_shape=jax.ShapeDtypeStruct((m, d), F32),
        scratch_shapes=[pltpu.VMEM((tm, d), F32), pltpu.VMEM((tm, d), BF16),
                        pltpu.SemaphoreType.DMA((2,))],
        compiler_params=_params(vmem),
        name="ffn",
    )(h, g.reshape(1, d), w_in, w_in, w_out, fg.reshape(1, d))


def _conv_in_body(h_ref, g_ref, wb_ref, wc_ref, wh_ref, cw_ref, y_ref, xn_ref, carry_ref,
                  *, tiles_per_seq):
    i = pl.program_id(0)
    j = pl.program_id(1)

    @pl.when(j == 0)
    def _():
        _rms_rows(h_ref, g_ref, xn_ref)

    xn = xn_ref[...]
    b = jnp.dot(xn, wb_ref[...], preferred_element_type=F32)
    c = jnp.dot(xn, wc_ref[...], preferred_element_type=F32)
    hh = jnp.dot(xn, wh_ref[...], preferred_element_type=F32)
    u = c * hh
    tm = u.shape[0]

    @pl.when(i % tiles_per_seq == 0)
    def _():
        carry_ref[j] = jnp.zeros(carry_ref.shape[1:], F32)

    prev = carry_ref[j]
    carry_ref[j] = u[tm - 8:, :]
    row = lax.broadcasted_iota(jnp.int32, u.shape, 0)
    u1 = jnp.where(row < 1, prev[7:8, :], pltpu.roll(u, 1, 0))
    u2 = jnp.where(row < 2, jnp.where(row < 1, prev[6:7, :], prev[7:8, :]), pltpu.roll(u, 2, 0))
    cw = cw_ref[...]
    conv = cw[0:1, :] * u2 + cw[1:2, :] * u1 + cw[2:3, :] * u
    y_ref[...] = (b * conv).astype(BF16)


def _conv_in(h, g, w_in, conv_w, *, seq, tm=512, tc=512):
    m, d = h.shape
    n_j = d // tc
    assert m % tm == 0 and seq % tm == 0 and d % tc == 0
    vmem = tm * d * 4 + tm * d * 2 + 2 * 3 * d * tc * 2 + 2 * tm * tc * 2 + n_j * 8 * tc * 4
    return pl.pallas_call(
        functools.partial(_conv_in_body, tiles_per_seq=seq // tm),
        grid=(m // tm, n_j),
        in_specs=[
            pl.BlockSpec((tm, d), lambda i, j: (i, 0), pipeline_mode=pl.Buffered(1)),
            pl.BlockSpec((1, d), lambda i, j: (0, 0)),
            pl.BlockSpec((d, tc), lambda i, j: (0, j)),
            pl.BlockSpec((d, tc), lambda i, j: (0, j + n_j)),
            pl.BlockSpec((d, tc), lambda i, j: (0, j + 2 * n_j)),
            pl.BlockSpec((CONV_WIDTH, tc), lambda i, j: (0, j)),
        ],
        out_specs=pl.BlockSpec((tm, tc), lambda i, j: (i, j)),
        out_shape=jax.ShapeDtypeStruct((m, d), BF16),
        scratch_shapes=[pltpu.VMEM((tm, d), BF16), pltpu.VMEM((n_j, 8, tc), F32)],
        compiler_params=_params(vmem),
        name="conv_in",
    )(h, g.reshape(1, d), w_in, w_in, w_in, conv_w)


def _proj_body(h_ref, g_ref, w_ref, *rest, n_q_tiles):
    *rope_refs, out_ref, xn_ref = rest
    j = pl.program_id(1)

    @pl.when(j == 0)
    def _():
        _rms_rows(h_ref, g_ref, xn_ref)

    t = jnp.dot(xn_ref[...], w_ref[...], preferred_element_type=F32)
    if not rope_refs:
        out_ref[...] = t.astype(BF16)
        return
    cos_ref, sin_ref = rope_refs
    cos = cos_ref[...]
    sin = sin_ref[...]
    scale = jnp.where(j < n_q_tiles, RET_QK_DIM ** -0.5, 1.0)
    half = RET_QK_DIM // 2
    for hd in range(t.shape[1] // RET_QK_DIM):
        lo = hd * RET_QK_DIM
        t1 = t[:, lo:lo + half]
        t2 = t[:, lo + half:lo + 2 * half]
        out_ref[:, lo:lo + half] = ((t1 * cos - t2 * sin) * scale).astype(BF16)
        out_ref[:, lo + half:lo + 2 * half] = ((t1 * sin + t2 * cos) * scale).astype(BF16)


def _proj(h, g, w, rope_tables=(), *, col0, n, seq, tm=1024, tn=1024):
    m, d = h.shape
    assert m % tm == 0 and n % tn == 0 and col0 % tn == 0 and seq % tm == 0
    half = RET_QK_DIM // 2
    assert half == LANES and tn % RET_QK_DIM == 0
    j0 = col0 // tn
    tiles_per_seq = seq // tm
    vmem = tm * d * 4 + tm * d * 2 + 2 * d * tn * 2 + 2 * tm * tn * 2 + 4 * tm * half * 4
    table_spec = pl.BlockSpec((tm, half), lambda i, j: (i % tiles_per_seq, 0))
    return pl.pallas_call(
        functools.partial(_proj_body, n_q_tiles=D_MODEL // tn),
        grid=(m // tm, n // tn),
        in_specs=[
            pl.BlockSpec((tm, d), lambda i, j: (i, 0), pipeline_mode=pl.Buffered(1)),
            pl.BlockSpec((1, d), lambda i, j: (0, 0)),
            pl.BlockSpec((d, tn), lambda i, j: (0, j + j0)),
        ] + [table_spec] * len(rope_tables),
        out_specs=pl.BlockSpec((tm, tn), lambda i, j: (i, j)),
        out_shape=jax.ShapeDtypeStruct((m, n), BF16),
        scratch_shapes=[pltpu.VMEM((tm, d), BF16)],
        compiler_params=_params(vmem),
        name="ret_proj_rope" if rope_tables else "ret_proj",
    )(h, g.reshape(1, d), w, *rope_tables)


def _ret_body(lg_ref, q_ref, k_ref, v_ref, gate_ref, gn_ref, out_ref, state_ref, *, t_blk):
    hd = pl.program_id(1)
    t = pl.program_id(2)
    lg = lg_ref[hd]

    @pl.when(t == 0)
    def _():
        state_ref[...] = jnp.zeros_like(state_ref)

    row = lax.broadcasted_iota(jnp.int32, (t_blk, t_blk), 0)
    col = lax.broadcasted_iota(jnp.int32, (t_blk, t_blk), 1)
    shift = CHUNK.bit_length() - 1
    rc = lax.shift_right_logical(row, shift)
    cc = lax.shift_right_logical(col, shift)
    diff = row - col
    dist = jnp.where(rc == cc, jnp.abs(diff), diff)
    visible = cc <= rc
    dist = jnp.where(visible, dist, 0).astype(F32)
    dmat = jnp.where(visible, jnp.exp(lg * dist), 0.0)

    q = q_ref[...]
    k = k_ref[...]
    v = v_ref[...]
    s = lax.dot_general(q, k, (((1,), (1,)), ((), ())), preferred_element_type=F32)
    p = (s * dmat).astype(BF16)
    o = jnp.dot(p, v, preferred_element_type=F32)

    idx = lax.broadcasted_iota(jnp.int32, (t_blk, 1), 0).astype(F32)
    q_dec = jnp.exp(lg * (idx + 1.0))
    k_dec = jnp.exp(lg * (t_blk - 1.0 - idx))
    qx = (q.astype(F32) * q_dec).astype(BF16)
    kxt = (k.astype(F32) * k_dec).T.astype(BF16)
    st = state_ref[...]
    o = o + jnp.dot(qx, st.astype(BF16), preferred_element_type=F32)
    blk_dec = jnp.exp(jnp.full((1, RET_V_DIM), t_blk, F32) * lg)
    state_ref[...] = blk_dec * st + jnp.dot(kxt, v, preferred_element_type=F32)

    mu = jnp.mean(o, axis=-1, keepdims=True)
    dlt = o - mu
    var = jnp.mean(dlt * dlt, axis=-1, keepdims=True)
    y = (dlt * lax.rsqrt(var + GN_EPS)) * gn_ref[...]
    out_ref[...] = (_silu(gate_ref[...].astype(F32)) * y).astype(BF16)


def _retention(qk, vg, gn_g, log_gamma, *, bsz, seq, t_blk=256):
    m = qk.shape[0]
    nt = seq // t_blk
    assert seq % t_blk == 0 and t_blk % CHUNK == 0
    return pl.pallas_call(
        functools.partial(_ret_body, t_blk=t_blk),
        grid=(bsz, RET_HEADS, nt),
        in_specs=[
            pl.BlockSpec(memory_space=pltpu.SMEM),
            pl.BlockSpec((t_blk, RET_QK_DIM), lambda b, h, t: (b * nt + t, h)),
            pl.BlockSpec((t_blk, RET_QK_DIM), lambda b, h, t: (b * nt + t, h + RET_HEADS)),
            pl.BlockSpec((t_blk, RET_V_DIM), lambda b, h, t: (b * nt + t, h)),
            pl.BlockSpec((t_blk, RET_V_DIM), lambda b, h, t: (b * nt + t, h + RET_HEADS)),
            pl.BlockSpec((1, RET_V_DIM), lambda b, h, t: (0, h)),
        ],
        out_specs=pl.BlockSpec((t_blk, RET_V_DIM), lambda b, h, t: (b * nt + t, h)),
        out_shape=jax.ShapeDtypeStruct((m, RET_HEADS * RET_V_DIM), BF16),
        scratch_shapes=[pltpu.VMEM((RET_QK_DIM, RET_V_DIM), F32)],
        compiler_params=pltpu.CompilerParams(
            dimension_semantics=("arbitrary", "arbitrary", "arbitrary")),
        name="retention",
    )(log_gamma, qk, qk, vg, vg, gn_g.reshape(1, -1))


def _out_proj_body(y_ref, w_ref, h_ref, out_ref):
    out_ref[...] = h_ref[...] + jnp.dot(y_ref[...], w_ref[...], preferred_element_type=F32)


def _out_proj(y, w, h, *, tm, tn, name):
    m, kdim = y.shape
    d = w.shape[1]
    assert m % tm == 0 and d % tn == 0
    vmem = 2 * tm * kdim * 2 + 2 * kdim * tn * 2 + 4 * tm * tn * 4
    return pl.pallas_call(
        _out_proj_body,
        grid=(m // tm, d // tn),
        in_specs=[
            pl.BlockSpec((tm, kdim), lambda i, j: (i, 0)),
            pl.BlockSpec((kdim, tn), lambda i, j: (0, j)),
            pl.BlockSpec((tm, tn), lambda i, j: (i, j)),
        ],
        out_specs=pl.BlockSpec((tm, tn), lambda i, j: (i, j)),
        out_shape=jax.ShapeDtypeStruct((m, d), F32),
        compiler_params=_params(vmem),
        name=name,
    )(y, w, h)


def kernel(x, l0_norm_ffn1, l0_ffn1_w_in, l0_ffn1_w_out, l0_norm_mix, l0_conv_w_in, l0_conv_w, l0_conv_w_out, l0_norm_ffn2, l0_ffn2_w_in, l0_ffn2_w_out, l1_norm_ffn1, l1_ffn1_w_in, l1_ffn1_w_out, l1_norm_mix, l1_ret_w_in, l1_ret_gn, l1_ret_w_out, l1_norm_ffn2, l1_ffn2_w_in, l1_ffn2_w_out, final_norm):
    bsz, seq, d = x.shape
    h = x.reshape(bsz * seq, d)
    bf = lambda w: w.astype(BF16)

    h = _ffn(h, l0_norm_ffn1, bf(l0_ffn1_w_in), bf(l0_ffn1_w_out))
    y = _conv_in(h, l0_norm_mix, bf(l0_conv_w_in), l0_conv_w, seq=seq)
    h = _out_proj(y, bf(l0_conv_w_out), h, tm=1024, tn=512, name="conv_out_proj")
    h = _ffn(h, l0_norm_ffn2, bf(l0_ffn2_w_in), bf(l0_ffn2_w_out))

    h = _ffn(h, l1_norm_ffn1, bf(l1_ffn1_w_in), bf(l1_ffn1_w_out))

    half = RET_QK_DIM // 2
    inv = ROPE_BASE ** (-jnp.arange(half, dtype=F32) / half)
    ang = jnp.arange(seq, dtype=jnp.int32).astype(F32)[:, None] * inv[None, :]
    cos, sin = jnp.cos(ang), jnp.sin(ang)
    log_gamma = jnp.log1p(-jnp.exp2(-5.0 - jnp.arange(RET_HEADS, dtype=F32)))

    w_ret = bf(l1_ret_w_in)
    qk = _proj(h, l1_norm_mix, w_ret, (cos, sin), col0=0, n=2 * d, seq=seq)
    vg = _proj(h, l1_norm_mix, w_ret, col0=2 * d, n=4 * d, seq=seq)
    y = _retention(qk, vg, l1_ret_gn, log_gamma, bsz=bsz, seq=seq)
    h = _out_proj(y, bf(l1_ret_w_out), h, tm=512, tn=512, name="ret_out_proj")

    h = _ffn(h, l1_norm_ffn2, bf(l1_ffn2_w_in), bf(l1_ffn2_w_out), final_norm)
    return h.reshape(bsz, seq, d)
```

```python
import functools

import jax
import jax.numpy as jnp
from jax import lax
from jax.experimental import pallas as pl
from jax.experimental.pallas import tpu as pltpu

D_MODEL = 4096
D_FF = 11008
CHUNK = 64
CONV_WIDTH = 3
RET_HEADS = 16
RET_QK_DIM = D_MODEL // RET_HEADS
RET_V_DIM = 2 * D_MODEL // RET_HEADS
ROPE_BASE = 10000.0
EPS = 1e-6
GN_EPS = 1e-5

F32 = jnp.float32
BF16 = jnp.bfloat16

V7X_VMEM_LIMIT_BYTES = 60000 * 1024
LANES = 128


def _params(vmem_bytes):
    assert vmem_bytes <= V7X_VMEM_LIMIT_BYTES, vmem_bytes
    return pltpu.CompilerParams(
        dimension_semantics=("arbitrary", "arbitrary"),
        vmem_limit_bytes=V7X_VMEM_LIMIT_BYTES,
    )


NORM_ROWS = 16


def _rms(x, g):
    ms = jnp.mean(x * x, axis=-1, keepdims=True)
    return (x * lax.rsqrt(ms + EPS)) * g


def _rms_rows(src_ref, g_ref, dst_ref):
    g = g_ref[...]

    def step(r, carry):
        rows = pl.ds(pl.multiple_of(r * NORM_ROWS, NORM_ROWS), NORM_ROWS)
        dst_ref[rows, :] = _rms(src_ref[rows, :], g).astype(dst_ref.dtype)
        return carry

    lax.fori_loop(0, src_ref.shape[0] // NORM_ROWS, step, 0)


def _silu(x):
    return x * jax.nn.sigmoid(x)


def _row_tile_copy(hbm_ref, vmem_ref, sem, *, to_vmem):
    tm = vmem_ref.shape[0]
    rows = pl.ds(pl.multiple_of(pl.program_id(0) * tm, tm), tm)
    if to_vmem:
        return pltpu.make_async_copy(hbm_ref.at[rows, :], vmem_ref, sem)
    return pltpu.make_async_copy(vmem_ref, hbm_ref.at[rows, :], sem)


def _ffn_body(h_hbm, g_ref, wg_ref, wu_ref, wo_ref, fg_ref, out_hbm, acc_ref, xn_ref, sem,
              *, n_j, final):
    j = pl.program_id(1)

    @pl.when(j == 0)
    def _():
        load = _row_tile_copy(h_hbm, acc_ref, sem.at[0], to_vmem=True)
        load.start()
        load.wait()
        _rms_rows(acc_ref, g_ref, xn_ref)

    xn = xn_ref[...]
    gate = jnp.dot(xn, wg_ref[...].astype(BF16), preferred_element_type=F32)
    up = jnp.dot(xn, wu_ref[...].astype(BF16), preferred_element_type=F32)
    act = (0.5 * _silu(gate) * up).astype(BF16)
    acc_ref[...] += jnp.dot(act, wo_ref[...].astype(BF16), preferred_element_type=F32)

    @pl.when(j == n_j - 1)
    def _():
        if final:
            _rms_rows(acc_ref, fg_ref, acc_ref)
        store = _row_tile_copy(out_hbm, acc_ref, sem.at[1], to_vmem=False)
        store.start()
        store.wait()


def _ffn(h, g, w_in, w_out, final_g=None, *, tm=1024, tf=256):
    m, d = h.shape
    d_ff = w_out.shape[0]
    n_j = d_ff // tf
    assert d_ff % tf == 0 and m % tm == 0
    final = final_g is not None
    fg = final_g if final else g
    vmem = tm * d * 4 + tm * d * 2 + 2 * 3 * d * tf * 4
    return pl.pallas_call(
        functools.partial(_ffn_body, n_j=n_j, final=final),
        grid=(m // tm, n_j),
        in_specs=[
            pl.BlockSpec(memory_space=pl.ANY),
            pl.BlockSpec((1, d), lambda i, j: (0, 0)),
            pl.BlockSpec((d, tf), lambda i, j: (0, j)),
            pl.BlockSpec((d, tf), lambda i, j: (0, j + n_j)),
            pl.BlockSpec((tf, d), lambda i, j: (j, 0)),
            pl.BlockSpec((1, d), lambda i, j: (0, 0)),
        ],
        out_specs=pl.BlockSpec(memory_space=pl.ANY),
        out_shape=jax.ShapeDtypeStruct((m, d), F32),
        scratch_shapes=[pltpu.VMEM((tm, d), F32), pltpu.VMEM((tm, d), BF16),
                        pltpu.SemaphoreType.DMA((2,))],
        compiler_params=_params(vmem),
        name="ffn",
    )(h, g.reshape(1, d), w_in, w_in, w_out, fg.reshape(1, d))


def _conv_in_body(h_ref, g_ref, wb_ref, wc_ref, wh_ref, cw_ref, y_ref, xn_ref, carry_ref,
                  *, tiles_per_seq):
    i = pl.program_id(0)
    j = pl.program_id(1)

    @pl.when(j == 0)
    def _():
        _rms_rows(h_ref, g_ref, xn_ref)

    xn = xn_ref[...]
    b = jnp.dot(xn, wb_ref[...], preferred_element_type=F32)
    c = jnp.dot(xn, wc_ref[...], preferred_element_type=F32)
    hh = jnp.dot(xn, wh_ref[...], preferred_element_type=F32)
    u = c * hh
    tm = u.shape[0]

    @pl.when(i % tiles_per_seq == 0)
    def _():
        carry_ref[j] = jnp.zeros(carry_ref.shape[1:], F32)

    prev = carry_ref[j]
    carry_ref[j] = u[tm - 8:, :]
    row = lax.broadcasted_iota(jnp.int32, u.shape, 0)
    u1 = jnp.where(row < 1, prev[7:8, :], pltpu.roll(u, 1, 0))
    u2 = jnp.where(row < 2, jnp.where(row < 1, prev[6:7, :], prev[7:8, :]), pltpu.roll(u, 2, 0))
    cw = cw_ref[...]
    conv = cw[0:1, :] * u2 + cw[1:2, :] * u1 + cw[2:3, :] * u
    y_ref[...] = (b * conv).astype(BF16)


def _conv_in(h, g, w_in, conv_w, *, seq, tm=512, tc=512):
    m, d = h.shape
    n_j = d // tc
    assert m % tm == 0 and seq % tm == 0 and d % tc == 0
    vmem = tm * d * 4 + tm * d * 2 + 2 * 3 * d * tc * 2 + 2 * tm * tc * 2 + n_j * 8 * tc * 4
    return pl.pallas_call(
        functools.partial(_conv_in_body, tiles_per_seq=seq // tm),
        grid=(m // tm, n_j),
        in_specs=[
            pl.BlockSpec((tm, d), lambda i, j: (i, 0), pipeline_mode=pl.Buffered(1)),
            pl.BlockSpec((1, d), lambda i, j: (0, 0)),
            pl.BlockSpec((d, tc), lambda i, j: (0, j)),
            pl.BlockSpec((d, tc), lambda i, j: (0, j + n_j)),
            pl.BlockSpec((d, tc), lambda i, j: (0, j + 2 * n_j)),
            pl.BlockSpec((CONV_WIDTH, tc), lambda i, j: (0, j)),
        ],
        out_specs=pl.BlockSpec((tm, tc), lambda i, j: (i, j)),
        out_shape=jax.ShapeDtypeStruct((m, d), BF16),
        scratch_shapes=[pltpu.VMEM((tm, d), BF16), pltpu.VMEM((n_j, 8, tc), F32)],
        compiler_params=_params(vmem),
        name="conv_in",
    )(h, g.reshape(1, d), w_in, w_in, w_in, conv_w)


def _proj_body(h_ref, g_ref, w_ref, *rest, n_q_tiles):
    *rope_refs, out_ref, xn_ref = rest
    j = pl.program_id(1)

    @pl.when(j == 0)
    def _():
        _rms_rows(h_ref, g_ref, xn_ref)

    t = jnp.dot(xn_ref[...], w_ref[...], preferred_element_type=F32)
    if not rope_refs:
        out_ref[...] = t.astype(BF16)
        return
    cos_ref, sin_ref = rope_refs
    cos = cos_ref[...]
    sin = sin_ref[...]
    scale = jnp.where(j < n_q_tiles, RET_QK_DIM ** -0.5, 1.0)
    half = RET_QK_DIM // 2
    for hd in range(t.shape[1] // RET_QK_DIM):
        lo = hd * RET_QK_DIM
        t1 = t[:, lo:lo + half]
        t2 = t[:, lo + half:lo + 2 * half]
        out_ref[:, lo:lo + half] = ((t1 * cos - t2 * sin) * scale).astype(BF16)
        out_ref[:, lo + half:lo + 2 * half] = ((t1 * sin + t2 * cos) * scale).astype(BF16)


def _proj(h, g, w, rope_tables=(), *, col0, n, seq, tm=1024, tn=1024):
    m, d = h.shape
    assert m % tm == 0 and n % tn == 0 and col0 % tn == 0 and seq % tm == 0
    half = RET_QK_DIM // 2
    assert half == LANES and tn % RET_QK_DIM == 0
    j0 = col0 // tn
    tiles_per_seq = seq // tm
    vmem = tm * d * 4 + tm * d * 2 + 2 * d * tn * 2 + 2 * tm * tn * 2 + 4 * tm * half * 4
    table_spec = pl.BlockSpec((tm, half), lambda i, j: (i % tiles_per_seq, 0))
    return pl.pallas_call(
        functools.partial(_proj_body, n_q_tiles=D_MODEL // tn),
        grid=(m // tm, n // tn),
        in_specs=[
            pl.BlockSpec((tm, d), lambda i, j: (i, 0), pipeline_mode=pl.Buffered(1)),
            pl.BlockSpec((1, d), lambda i, j: (0, 0)),
            pl.BlockSpec((d, tn), lambda i, j: (0, j + j0)),
        ] + [table_spec] * len(rope_tables),
        out_specs=pl.BlockSpec((tm, tn), lambda i, j: (i, j)),
        out_shape=jax.ShapeDtypeStruct((m, n), BF16),
        scratch_shapes=[pltpu.VMEM((tm, d), BF16)],
        compiler_params=_params(vmem),
        name="ret_proj_rope" if rope_tables else "ret_proj",
    )(h, g.reshape(1, d), w, *rope_tables)


def _ret_body(lg_ref, q_ref, k_ref, v_ref, gate_ref, gn_ref, out_ref,
              state_ref, dmat_ref, qdec_ref, kdec_ref, *, t_blk, heads):
    hg = pl.program_id(1)
    t = pl.program_id(2)

    @pl.when(t == 0)
    def _():
        state_ref[...] = jnp.zeros_like(state_ref)
        row = lax.broadcasted_iota(jnp.int32, (t_blk, t_blk), 0)
        col = lax.broadcasted_iota(jnp.int32, (t_blk, t_blk), 1)
        shift = CHUNK.bit_length() - 1
        rc = lax.shift_right_logical(row, shift)
        cc = lax.shift_right_logical(col, shift)
        diff = row - col
        visible = cc <= rc
        dist = jnp.where(visible, jnp.where(rc == cc, jnp.abs(diff), diff), 0).astype(F32)
        idx = lax.broadcasted_iota(jnp.int32, (t_blk, RET_QK_DIM), 0).astype(F32)
        for hh in range(heads):
            lg = lg_ref[hg * heads + hh]
            dmat_ref[hh] = jnp.where(visible, jnp.exp(lg * dist), 0.0)
            qdec_ref[hh] = jnp.exp(lg * (idx + 1.0))
            kdec_ref[hh] = jnp.exp(lg * (t_blk - 1.0 - idx))

    for hh in range(heads):
        qk_cols = slice(hh * RET_QK_DIM, (hh + 1) * RET_QK_DIM)
        v_cols = slice(hh * RET_V_DIM, (hh + 1) * RET_V_DIM)
        q = q_ref[:, qk_cols]
        k = k_ref[:, qk_cols]
        v = v_ref[:, v_cols]
        s = lax.dot_general(q, k, (((1,), (1,)), ((), ())), preferred_element_type=F32)
        p = (s * dmat_ref[hh]).astype(BF16)
        o = jnp.dot(p, v, preferred_element_type=F32)

        qx = (q.astype(F32) * qdec_ref[hh]).astype(BF16)
        kxt = (k.astype(F32) * kdec_ref[hh]).T.astype(BF16)
        st = state_ref[hh]
        o = o + jnp.dot(qx, st.astype(BF16), preferred_element_type=F32)
        blk_dec = qdec_ref[hh, t_blk - 1:t_blk, 0:1]
        state_ref[hh] = blk_dec * st + jnp.dot(kxt, v, preferred_element_type=F32)

        mu = jnp.mean(o, axis=-1, keepdims=True)
        dlt = o - mu
        var = jnp.mean(dlt * dlt, axis=-1, keepdims=True)
        y = (dlt * lax.rsqrt(var + GN_EPS)) * gn_ref[:, v_cols]
        out_ref[:, v_cols] = (_silu(gate_ref[:, v_cols].astype(F32)) * y).astype(BF16)


def _retention(qk, vg, gn_g, log_gamma, *, bsz, seq, t_blk=256, heads=4):
    m = qk.shape[0]
    nt = seq // t_blk
    n_hg = RET_HEADS // heads
    assert seq % t_blk == 0 and t_blk % CHUNK == 0 and RET_HEADS % heads == 0
    qk_w = heads * RET_QK_DIM
    v_w = heads * RET_V_DIM
    return pl.pallas_call(
        functools.partial(_ret_body, t_blk=t_blk, heads=heads),
        grid=(bsz, n_hg, nt),
        in_specs=[
            pl.BlockSpec(memory_space=pltpu.SMEM),
            pl.BlockSpec((t_blk, qk_w), lambda b, h, t: (b * nt + t, h)),
            pl.BlockSpec((t_blk, qk_w), lambda b, h, t: (b * nt + t, h + n_hg)),
            pl.BlockSpec((t_blk, v_w), lambda b, h, t: (b * nt + t, h)),
            pl.BlockSpec((t_blk, v_w), lambda b, h, t: (b * nt + t, h + n_hg)),
            pl.BlockSpec((1, v_w), lambda b, h, t: (0, h)),
        ],
        out_specs=pl.BlockSpec((t_blk, v_w), lambda b, h, t: (b * nt + t, h)),
        out_shape=jax.ShapeDtypeStruct((m, RET_HEADS * RET_V_DIM), BF16),
        scratch_shapes=[pltpu.VMEM((heads, RET_QK_DIM, RET_V_DIM), F32),
                        pltpu.VMEM((heads, t_blk, t_blk), F32),
                        pltpu.VMEM((heads, t_blk, RET_QK_DIM), F32),
                        pltpu.VMEM((heads, t_blk, RET_QK_DIM), F32)],
        compiler_params=pltpu.CompilerParams(
            dimension_semantics=("arbitrary", "arbitrary", "arbitrary")),
        name="retention",
    )(log_gamma, qk, qk, vg, vg, gn_g.reshape(1, -1))


def _out_proj_body(y_ref, w_ref, h_ref, out_ref):
    out_ref[...] = h_ref[...] + jnp.dot(y_ref[...], w_ref[...], preferred_element_type=F32)


def _out_proj(y, w, h, *, tm, tn, name):
    m, kdim = y.shape
    d = w.shape[1]
    assert m % tm == 0 and d % tn == 0
    vmem = 2 * tm * kdim * 2 + 2 * kdim * tn * 2 + 4 * tm * tn * 4
    return pl.pallas_call(
        _out_proj_body,
        grid=(m // tm, d // tn),
        in_specs=[
            pl.BlockSpec((tm, kdim), lambda i, j: (i, 0)),
            pl.BlockSpec((kdim, tn), lambda i, j: (0, j)),
            pl.BlockSpec((tm, tn), lambda i, j: (i, j)),
        ],
        out_specs=pl.BlockSpec((tm, tn), lambda i, j: (i, j)),
        out_shape=jax.ShapeDtypeStruct((m, d), F32),
        compiler_params=_params(vmem),
        name=name,
    )(y, w, h)


def kernel(x, l0_norm_ffn1, l0_ffn1_w_in, l0_ffn1_w_out, l0_norm_mix, l0_conv_w_in, l0_conv_w, l0_conv_w_out, l0_norm_ffn2, l0_ffn2_w_in, l0_ffn2_w_out, l1_norm_ffn1, l1_ffn1_w_in, l1_ffn1_w_out, l1_norm_mix, l1_ret_w_in, l1_ret_gn, l1_ret_w_out, l1_norm_ffn2, l1_ffn2_w_in, l1_ffn2_w_out, final_norm):
    bsz, seq, d = x.shape
    h = x.reshape(bsz * seq, d)
    bf = lambda w: w.astype(BF16)

    h = _ffn(h, l0_norm_ffn1, l0_ffn1_w_in, l0_ffn1_w_out)
    y = _conv_in(h, l0_norm_mix, bf(l0_conv_w_in), l0_conv_w, seq=seq)
    h = _out_proj(y, bf(l0_conv_w_out), h, tm=1024, tn=512, name="conv_out_proj")
    h = _ffn(h, l0_norm_ffn2, l0_ffn2_w_in, l0_ffn2_w_out)

    h = _ffn(h, l1_norm_ffn1, l1_ffn1_w_in, l1_ffn1_w_out)

    half = RET_QK_DIM // 2
    inv = ROPE_BASE ** (-jnp.arange(half, dtype=F32) / half)
    ang = jnp.arange(seq, dtype=jnp.int32).astype(F32)[:, None] * inv[None, :]
    cos, sin = jnp.cos(ang), jnp.sin(ang)
    log_gamma = jnp.log1p(-jnp.exp2(-5.0 - jnp.arange(RET_HEADS, dtype=F32)))

    w_ret = bf(l1_ret_w_in)
    qk = _proj(h, l1_norm_mix, w_ret, (cos, sin), col0=0, n=2 * d, seq=seq)
    vg = _proj(h, l1_norm_mix, w_ret, col0=2 * d, n=4 * d, seq=seq)
    y = _retention(qk, vg, l1_ret_gn, log_gamma, bsz=bsz, seq=seq)
    h = _out_proj(y, bf(l1_ret_w_out), h, tm=512, tn=512, name="ret_out_proj")

    h = _ffn(h, l1_norm_ffn2, l1_ffn2_w_in, l1_ffn2_w_out, final_norm)
    return h.reshape(bsz, seq, d)
```

```python
import functools

import jax
import jax.numpy as jnp
from jax import lax
from jax.experimental import pallas as pl
from jax.experimental.pallas import tpu as pltpu

D_MODEL = 4096
D_FF = 11008
CHUNK = 64
CONV_WIDTH = 3
RET_HEADS = 16
RET_QK_DIM = D_MODEL // RET_HEADS
RET_V_DIM = 2 * D_MODEL // RET_HEADS
ROPE_BASE = 10000.0
EPS = 1e-6
GN_EPS = 1e-5

F32 = jnp.float32
BF16 = jnp.bfloat16

V7X_VMEM_LIMIT_BYTES = 60000 * 1024
LANES = 128


def _params(vmem_bytes):
    assert vmem_bytes <= V7X_VMEM_LIMIT_BYTES, vmem_bytes
    return pltpu.CompilerParams(
        dimension_semantics=("arbitrary", "arbitrary"),
        vmem_limit_bytes=V7X_VMEM_LIMIT_BYTES,
    )


NORM_ROWS = 16
NORM_UNROLL = 4
ROW_CHUNK = 128


def _rms(x, g):
    ms = jnp.mean(x * x, axis=-1, keepdims=True)
    return (x * lax.rsqrt(ms + EPS)) * g


def _rms_rows(src_ref, g_ref, dst_ref):
    g = g_ref[...]

    def step(r, carry):
        rows = pl.ds(pl.multiple_of(r * NORM_ROWS, NORM_ROWS), NORM_ROWS)
        dst_ref[rows, :] = _rms(src_ref[rows, :], g).astype(dst_ref.dtype)
        return carry

    lax.fori_loop(0, src_ref.shape[0] // NORM_ROWS, step, 0, unroll=NORM_UNROLL)


def _chunk_rows(c):
    return pl.ds(c * ROW_CHUNK, ROW_CHUNK)


def _tile_row0(tm):
    return pl.multiple_of(pl.program_id(0) * tm, tm)


def _stream_rms(h_hbm, g_ref, xn_ref, stage_ref, sem):
    tm = xn_ref.shape[0]
    n_c = tm // ROW_CHUNK
    row0 = _tile_row0(tm)

    def copy(c):
        src = h_hbm.at[pl.ds(row0 + c * ROW_CHUNK, ROW_CHUNK), :]
        return pltpu.make_async_copy(src, stage_ref.at[c % 2], sem.at[c % 2])

    copy(0).start()
    for c in range(n_c):
        if c + 1 < n_c:
            copy(c + 1).start()
        copy(c).wait()
        _rms_rows(stage_ref.at[c % 2], g_ref, xn_ref.at[_chunk_rows(c), :])


def _stream_scratch(tm, d):
    assert tm % ROW_CHUNK == 0
    return [pltpu.VMEM((tm, d), BF16), pltpu.VMEM((2, ROW_CHUNK, d), F32),
            pltpu.SemaphoreType.DMA((2,))]


def _silu(x):
    return x * jax.nn.sigmoid(x)


OUT_COL_CHUNKS = 4


def _ffn_body(h_hbm, g_ref, wg_ref, wu_ref, wo_ref, fg_ref, out_hbm, acc_ref, xn_ref, sem,
              *, n_j, final):
    j = pl.program_id(1)
    tm, d = acc_ref.shape
    n_c = tm // ROW_CHUNK
    row0 = _tile_row0(tm)

    def row_copy(c, *, load):
        hbm_rows = pl.ds(row0 + c * ROW_CHUNK, ROW_CHUNK)
        vm = acc_ref.at[_chunk_rows(c), :]
        if load:
            return pltpu.make_async_copy(h_hbm.at[hbm_rows, :], vm, sem.at[c])
        return pltpu.make_async_copy(vm, out_hbm.at[hbm_rows, :], sem.at[c])

    def col_copy(c):
        cols = pl.ds(c * (d // OUT_COL_CHUNKS), d // OUT_COL_CHUNKS)
        return pltpu.make_async_copy(acc_ref.at[:, cols], out_hbm.at[pl.ds(row0, tm), cols],
                                     sem.at[c])

    @pl.when(j == 0)
    def _():
        for c in range(n_c):
            row_copy(c, load=True).start()
        for c in range(n_c):
            row_copy(c, load=True).wait()
            _rms_rows(acc_ref.at[_chunk_rows(c), :], g_ref, xn_ref.at[_chunk_rows(c), :])

    def step(last):
        xn = xn_ref[...]
        gate = jnp.dot(xn, wg_ref[...].astype(BF16), preferred_element_type=F32)
        up = jnp.dot(xn, wu_ref[...].astype(BF16), preferred_element_type=F32)
        act = (0.5 * _silu(gate) * up).astype(BF16)
        wo = wo_ref[...].astype(BF16)
        if not last:
            acc_ref[...] += jnp.dot(act, wo, preferred_element_type=F32)
        elif final:
            acc_ref[...] += jnp.dot(act, wo, preferred_element_type=F32)
            for c in range(n_c):
                rows = acc_ref.at[_chunk_rows(c), :]
                _rms_rows(rows, fg_ref, rows)
                row_copy(c, load=False).start()
            for c in range(n_c):
                row_copy(c, load=False).wait()
        else:
            dc = d // OUT_COL_CHUNKS
            for c in range(OUT_COL_CHUNKS):
                cols = slice(c * dc, (c + 1) * dc)
                acc_ref[:, cols] += jnp.dot(act, wo[:, cols], preferred_element_type=F32)
                col_copy(c).start()
            for c in range(OUT_COL_CHUNKS):
                col_copy(c).wait()

    @pl.when(j < n_j - 1)
    def _():
        step(False)

    @pl.when(j == n_j - 1)
    def _():
        step(True)


def _ffn(h, g, w_in, w_out, final_g=None, *, tm=1024, tf=256):
    m, d = h.shape
    d_ff = w_out.shape[0]
    n_j = d_ff // tf
    assert d_ff % tf == 0 and m % tm == 0 and tm % ROW_CHUNK == 0
    assert d % (OUT_COL_CHUNKS * LANES) == 0 and OUT_COL_CHUNKS <= tm // ROW_CHUNK
    final = final_g is not None
    fg = final_g if final else g
    vmem = tm * d * 4 + tm * d * 2 + 2 * 3 * d * tf * 4
    return pl.pallas_call(
        functools.partial(_ffn_body, n_j=n_j, final=final),
        grid=(m // tm, n_j),
        in_specs=[
            pl.BlockSpec(memory_space=pl.ANY),
            pl.BlockSpec((1, d), lambda i, j: (0, 0)),
            pl.BlockSpec((d, tf), lambda i, j: (0, j)),
            pl.BlockSpec((d, tf), lambda i, j: (0, j + n_j)),
            pl.BlockSpec((tf, d), lambda i, j: (j, 0)),
            pl.BlockSpec((1, d), lambda i, j: (0, 0)),
        ],
        out_specs=pl.BlockSpec(memory_space=pl.ANY),
        out_shape=jax.ShapeDtypeStruct((m, d), F32),
        scratch_shapes=[pltpu.VMEM((tm, d), F32), pltpu.VMEM((tm, d), BF16),
                        pltpu.SemaphoreType.DMA((tm // ROW_CHUNK,))],
        compiler_params=_params(vmem),
        name="ffn",
    )(h, g.reshape(1, d), w_in, w_in, w_out, fg.reshape(1, d))


def _conv_in_body(h_hbm, g_ref, wb_ref, wc_ref, wh_ref, cw_ref, y_ref,
                  xn_ref, stage_ref, sem, carry_ref, *, tiles_per_seq):
    i = pl.program_id(0)
    j = pl.program_id(1)

    @pl.when(j == 0)
    def _():
        _stream_rms(h_hbm, g_ref, xn_ref, stage_ref, sem)

    xn = xn_ref[...]
    b = jnp.dot(xn, wb_ref[...], preferred_element_type=F32)
    c = jnp.dot(xn, wc_ref[...], preferred_element_type=F32)
    hh = jnp.dot(xn, wh_ref[...], preferred_element_type=F32)
    u = c * hh
    tm = u.shape[0]

    @pl.when(i % tiles_per_seq == 0)
    def _():
        carry_ref[j] = jnp.zeros(carry_ref.shape[1:], F32)

    prev = carry_ref[j]
    carry_ref[j] = u[tm - 8:, :]
    row = lax.broadcasted_iota(jnp.int32, u.shape, 0)
    u1 = jnp.where(row < 1, prev[7:8, :], pltpu.roll(u, 1, 0))
    u2 = jnp.where(row < 2, jnp.where(row < 1, prev[6:7, :], prev[7:8, :]), pltpu.roll(u, 2, 0))
    cw = cw_ref[...]
    conv = cw[0:1, :] * u2 + cw[1:2, :] * u1 + cw[2:3, :] * u
    y_ref[...] = (b * conv).astype(BF16)


def _conv_in(h, g, w_in, conv_w, *, seq, tm=1024, tc=512):
    m, d = h.shape
    n_j = d // tc
    assert m % tm == 0 and seq % tm == 0 and d % tc == 0
    vmem = (2 * ROW_CHUNK * d * 4 + tm * d * 2 + 2 * 3 * d * tc * 2 + 2 * tm * tc * 2
            + n_j * 8 * tc * 4 + 6 * tm * tc * 4)
    return pl.pallas_call(
        functools.partial(_conv_in_body, tiles_per_seq=seq // tm),
        grid=(m // tm, n_j),
        in_specs=[
            pl.BlockSpec(memory_space=pl.ANY),
            pl.BlockSpec((1, d), lambda i, j: (0, 0)),
            pl.BlockSpec((d, tc), lambda i, j: (0, j)),
            pl.BlockSpec((d, tc), lambda i, j: (0, j + n_j)),
            pl.BlockSpec((d, tc), lambda i, j: (0, j + 2 * n_j)),
            pl.BlockSpec((CONV_WIDTH, tc), lambda i, j: (0, j)),
        ],
        out_specs=pl.BlockSpec((tm, tc), lambda i, j: (i, j)),
        out_shape=jax.ShapeDtypeStruct((m, d), BF16),
        scratch_shapes=_stream_scratch(tm, d) + [pltpu.VMEM((n_j, 8, tc), F32)],
        compiler_params=_params(vmem),
        name="conv_in",
    )(h, g.reshape(1, d), w_in, w_in, w_in, conv_w)


def _proj_body(h_hbm, g_ref, w_ref, *rest, n_q_tiles):
    *rope_refs, out_ref, xn_ref, stage_ref, sem = rest
    j = pl.program_id(1)

    @pl.when(j == 0)
    def _():
        _stream_rms(h_hbm, g_ref, xn_ref, stage_ref, sem)

    t = jnp.dot(xn_ref[...], w_ref[...].astype(BF16), preferred_element_type=F32)
    if not rope_refs:
        out_ref[...] = t.astype(BF16)
        return
    cos_ref, sin_ref = rope_refs
    cos = cos_ref[...]
    sin = sin_ref[...]
    scale = jnp.where(j < n_q_tiles, RET_QK_DIM ** -0.5, 1.0)
    half = RET_QK_DIM // 2
    for hd in range(t.shape[1] // RET_QK_DIM):
        lo = hd * RET_QK_DIM
        t1 = t[:, lo:lo + half]
        t2 = t[:, lo + half:lo + 2 * half]
        out_ref[:, lo:lo + half] = ((t1 * cos - t2 * sin) * scale).astype(BF16)
        out_ref[:, lo + half:lo + 2 * half] = ((t1 * sin + t2 * cos) * scale).astype(BF16)


def _proj(h, g, w, rope_tables=(), *, col0, n, seq, tm=1024, tn=1024):
    m, d = h.shape
    assert m % tm == 0 and n % tn == 0 and col0 % tn == 0 and seq % tm == 0
    half = RET_QK_DIM // 2
    assert half == LANES and tn % RET_QK_DIM == 0
    j0 = col0 // tn
    tiles_per_seq = seq // tm
    vmem = (2 * ROW_CHUNK * d * 4 + tm * d * 2 + 2 * d * tn * 4 + 2 * tm * tn * 2
            + 4 * tm * half * 4)
    table_spec = pl.BlockSpec((tm, half), lambda i, j: (i % tiles_per_seq, 0))
    return pl.pallas_call(
        functools.partial(_proj_body, n_q_tiles=D_MODEL // tn),
        grid=(m // tm, n // tn),
        in_specs=[
            pl.BlockSpec(memory_space=pl.ANY),
            pl.BlockSpec((1, d), lambda i, j: (0, 0)),
            pl.BlockSpec((d, tn), lambda i, j: (0, j + j0)),
        ] + [table_spec] * len(rope_tables),
        out_specs=pl.BlockSpec((tm, tn), lambda i, j: (i, j)),
        out_shape=jax.ShapeDtypeStruct((m, n), BF16),
        scratch_shapes=_stream_scratch(tm, d),
        compiler_params=_params(vmem),
        name="ret_proj_rope" if rope_tables else "ret_proj",
    )(h, g.reshape(1, d), w, *rope_tables)


def _ret_body(lg_ref, q_ref, k_ref, v_ref, gate_ref, gn_ref, out_ref,
              state_ref, dmat_ref, qdec_ref, kdec_ref, *, t_blk, heads):
    hg = pl.program_id(1)
    t = pl.program_id(2)

    @pl.when(t == 0)
    def _():
        state_ref[...] = jnp.zeros_like(state_ref)
        row = lax.broadcasted_iota(jnp.int32, (t_blk, t_blk), 0)
        col = lax.broadcasted_iota(jnp.int32, (t_blk, t_blk), 1)
        shift = CHUNK.bit_length() - 1
        rc = lax.shift_right_logical(row, shift)
        cc = lax.shift_right_logical(col, shift)
        diff = row - col
        visible = cc <= rc
        dist = jnp.where(visible, jnp.where(rc == cc, jnp.abs(diff), diff), 0).astype(F32)
        idx = lax.broadcasted_iota(jnp.int32, (t_blk, RET_QK_DIM), 0).astype(F32)
        for hh in range(heads):
            lg = lg_ref[hg * heads + hh]
            dmat_ref[hh] = jnp.where(visible, jnp.exp(lg * dist), 0.0)
            qdec_ref[hh] = jnp.exp(lg * (idx + 1.0))
            kdec_ref[hh] = jnp.exp(lg * (t_blk - 1.0 - idx))

    for hh in range(heads):
        qk_cols = slice(hh * RET_QK_DIM, (hh + 1) * RET_QK_DIM)
        v_cols = slice(hh * RET_V_DIM, (hh + 1) * RET_V_DIM)
        q = q_ref[:, qk_cols]
        k = k_ref[:, qk_cols]
        v = v_ref[:, v_cols]
        s = lax.dot_general(q, k, (((1,), (1,)), ((), ())), preferred_element_type=F32)
        p = (s * dmat_ref[hh]).astype(BF16)
        o = jnp.dot(p, v, preferred_element_type=F32)

        qx = (q.astype(F32) * qdec_ref[hh]).astype(BF16)
        kxt = (k.astype(F32) * kdec_ref[hh]).T.astype(BF16)
        st = state_ref[hh]
        o = o + jnp.dot(qx, st.astype(BF16), preferred_element_type=F32)
        blk_dec = qdec_ref[hh, t_blk - 1:t_blk, 0:1]
        state_ref[hh] = blk_dec * st + jnp.dot(kxt, v, preferred_element_type=F32)

        mu = jnp.mean(o, axis=-1, keepdims=True)
        dlt = o - mu
        var = jnp.mean(dlt * dlt, axis=-1, keepdims=True)
        y = (dlt * lax.rsqrt(var + GN_EPS)) * gn_ref[:, v_cols]
        out_ref[:, v_cols] = (_silu(gate_ref[:, v_cols].astype(F32)) * y).astype(BF16)


def _retention(qk, vg, gn_g, log_gamma, *, bsz, seq, t_blk=256, heads=4):
    m = qk.shape[0]
    nt = seq // t_blk
    n_hg = RET_HEADS // heads
    assert seq % t_blk == 0 and t_blk % CHUNK == 0 and RET_HEADS % heads == 0
    qk_w = heads * RET_QK_DIM
    v_w = heads * RET_V_DIM
    return pl.pallas_call(
        functools.partial(_ret_body, t_blk=t_blk, heads=heads),
        grid=(bsz, n_hg, nt),
        in_specs=[
            pl.BlockSpec(memory_space=pltpu.SMEM),
            pl.BlockSpec((t_blk, qk_w), lambda b, h, t: (b * nt + t, h)),
            pl.BlockSpec((t_blk, qk_w), lambda b, h, t: (b * nt + t, h + n_hg)),
            pl.BlockSpec((t_blk, v_w), lambda b, h, t: (b * nt + t, h)),
            pl.BlockSpec((t_blk, v_w), lambda b, h, t: (b * nt + t, h + n_hg)),
            pl.BlockSpec((1, v_w), lambda b, h, t: (0, h)),
        ],
        out_specs=pl.BlockSpec((t_blk, v_w), lambda b, h, t: (b * nt + t, h)),
        out_shape=jax.ShapeDtypeStruct((m, RET_HEADS * RET_V_DIM), BF16),
        scratch_shapes=[pltpu.VMEM((heads, RET_QK_DIM, RET_V_DIM), F32),
                        pltpu.VMEM((heads, t_blk, t_blk), F32),
                        pltpu.VMEM((heads, t_blk, RET_QK_DIM), F32),
                        pltpu.VMEM((heads, t_blk, RET_QK_DIM), F32)],
        compiler_params=pltpu.CompilerParams(
            dimension_semantics=("arbitrary", "arbitrary", "arbitrary")),
        name="retention",
    )(log_gamma, qk, qk, vg, vg, gn_g.reshape(1, -1))


def _out_proj_body(y_ref, w_ref, h_ref, out_ref):
    out_ref[...] = h_ref[...] + jnp.dot(y_ref[...], w_ref[...], preferred_element_type=F32)


def _out_proj(y, w, h, *, tm, tn, name):
    m, kdim = y.shape
    d = w.shape[1]
    assert m % tm == 0 and d % tn == 0
    vmem = 2 * tm * kdim * 2 + 2 * kdim * tn * 2 + 4 * tm * tn * 4
    return pl.pallas_call(
        _out_proj_body,
        grid=(m // tm, d // tn),
        in_specs=[
            pl.BlockSpec((tm, kdim), lambda i, j: (i, 0)),
            pl.BlockSpec((kdim, tn), lambda i, j: (0, j)),
            pl.BlockSpec((tm, tn), lambda i, j: (i, j)),
        ],
        out_specs=pl.BlockSpec((tm, tn), lambda i, j: (i, j)),
        out_shape=jax.ShapeDtypeStruct((m, d), F32),
        compiler_params=_params(vmem),
        name=name,
    )(y, w, h)


def kernel(x, l0_norm_ffn1, l0_ffn1_w_in, l0_ffn1_w_out, l0_norm_mix, l0_conv_w_in, l0_conv_w, l0_conv_w_out, l0_norm_ffn2, l0_ffn2_w_in, l0_ffn2_w_out, l1_norm_ffn1, l1_ffn1_w_in, l1_ffn1_w_out, l1_norm_mix, l1_ret_w_in, l1_ret_gn, l1_ret_w_out, l1_norm_ffn2, l1_ffn2_w_in, l1_ffn2_w_out, final_norm):
    bsz, seq, d = x.shape
    h = x.reshape(bsz * seq, d)
    bf = lambda w: w.astype(BF16)

    h = _ffn(h, l0_norm_ffn1, l0_ffn1_w_in, l0_ffn1_w_out)
    y = _conv_in(h, l0_norm_mix, bf(l0_conv_w_in), l0_conv_w, seq=seq)
    h = _out_proj(y, bf(l0_conv_w_out), h, tm=1024, tn=512, name="conv_out_proj")
    h = _ffn(h, l0_norm_ffn2, l0_ffn2_w_in, l0_ffn2_w_out)

    h = _ffn(h, l1_norm_ffn1, l1_ffn1_w_in, l1_ffn1_w_out)

    half = RET_QK_DIM // 2
    inv = ROPE_BASE ** (-jnp.arange(half, dtype=F32) / half)
    ang = jnp.arange(seq, dtype=jnp.int32).astype(F32)[:, None] * inv[None, :]
    cos, sin = jnp.cos(ang), jnp.sin(ang)
    log_gamma = jnp.log1p(-jnp.exp2(-5.0 - jnp.arange(RET_HEADS, dtype=F32)))

    qk = _proj(h, l1_norm_mix, l1_ret_w_in, (cos, sin), col0=0, n=2 * d, seq=seq, tn=512)
    vg = _proj(h, l1_norm_mix, l1_ret_w_in, col0=2 * d, n=4 * d, seq=seq)
    y = _retention(qk, vg, l1_ret_gn, log_gamma, bsz=bsz, seq=seq)
    h = _out_proj(y, bf(l1_ret_w_out), h, tm=512, tn=512, name="ret_out_proj")

    h = _ffn(h, l1_norm_ffn2, l1_ffn2_w_in, l1_ffn2_w_out, final_norm)
    return h.reshape(bsz, seq, d)
```

```python
import functools

import jax
import jax.numpy as jnp
from jax import lax
from jax.experimental import pallas as pl
from jax.experimental.pallas import tpu as pltpu

D_MODEL = 4096
D_FF = 11008
CHUNK = 64
CONV_WIDTH = 3
RET_HEADS = 16
RET_QK_DIM = D_MODEL // RET_HEADS
RET_V_DIM = 2 * D_MODEL // RET_HEADS
ROPE_BASE = 10000.0
EPS = 1e-6
GN_EPS = 1e-5

F32 = jnp.float32
BF16 = jnp.bfloat16

V7X_VMEM_LIMIT_BYTES = 60000 * 1024
LANES = 128


def _params(vmem_bytes):
    assert vmem_bytes <= V7X_VMEM_LIMIT_BYTES, vmem_bytes
    return pltpu.CompilerParams(
        dimension_semantics=("arbitrary", "arbitrary"),
        vmem_limit_bytes=V7X_VMEM_LIMIT_BYTES,
    )


NORM_ROWS = 16
NORM_UNROLL = 4
ROW_CHUNK = 128


def _rms(x, g):
    ms = jnp.mean(x * x, axis=-1, keepdims=True)
    return (x * lax.rsqrt(ms + EPS)) * g


def _rms_rows(src_ref, g_ref, dst_ref):
    g = g_ref[...]
    rows_per_step = NORM_ROWS * NORM_UNROLL
    assert src_ref.shape[0] % rows_per_step == 0

    def step(r, carry):
        base = pl.multiple_of(r * rows_per_step, rows_per_step)
        slabs = [pl.ds(base + u * NORM_ROWS, NORM_ROWS) for u in range(NORM_UNROLL)]
        xs = [src_ref[rows, :] for rows in slabs]
        for rows, x in zip(slabs, xs):
            dst_ref[rows, :] = _rms(x, g).astype(dst_ref.dtype)
        return carry

    lax.fori_loop(0, src_ref.shape[0] // rows_per_step, step, 0)


def _chunk_rows(c):
    return pl.ds(c * ROW_CHUNK, ROW_CHUNK)


def _tile_row0(tm):
    return pl.multiple_of(pl.program_id(0) * tm, tm)


def _stream_rms(h_hbm, g_ref, xn_ref, stage_ref, sem):
    tm = xn_ref.shape[0]
    n_c = tm // ROW_CHUNK
    row0 = _tile_row0(tm)

    def copy(c):
        src = h_hbm.at[pl.ds(row0 + c * ROW_CHUNK, ROW_CHUNK), :]
        return pltpu.make_async_copy(src, stage_ref.at[c % 2], sem.at[c % 2])

    copy(0).start()
    for c in range(n_c):
        if c + 1 < n_c:
            copy(c + 1).start()
        copy(c).wait()
        _rms_rows(stage_ref.at[c % 2], g_ref, xn_ref.at[_chunk_rows(c), :])


def _stream_scratch(tm, d):
    assert tm % ROW_CHUNK == 0
    return [pltpu.VMEM((tm, d), BF16), pltpu.VMEM((2, ROW_CHUNK, d), F32),
            pltpu.SemaphoreType.DMA((2,))]


def _silu(x):
    return x * jax.nn.sigmoid(x)


OUT_COL_CHUNKS = 4
SIDE_BLOCKS = 256
BF16_SUBLANES = 16


def _ffn_body(h_hbm, g_ref, wg_ref, wu_ref, wo_ref, fg_ref, *rest, n_j, final, n_side):
    side_in = rest[:n_side]
    out_hbm = rest[n_side]
    side_out = rest[n_side + 1:2 * n_side + 1]
    acc_ref, xn_ref, sem = rest[2 * n_side + 1:]
    j = pl.program_id(1)
    tm, d = acc_ref.shape
    n_c = tm // ROW_CHUNK
    row0 = _tile_row0(tm)

    def row_copy(c, *, load):
        hbm_rows = pl.ds(row0 + c * ROW_CHUNK, ROW_CHUNK)
        vm = acc_ref.at[_chunk_rows(c), :]
        if load:
            return pltpu.make_async_copy(h_hbm.at[hbm_rows, :], vm, sem.at[c])
        return pltpu.make_async_copy(vm, out_hbm.at[hbm_rows, :], sem.at[c])

    def col_copy(c):
        cols = pl.ds(c * (d // OUT_COL_CHUNKS), d // OUT_COL_CHUNKS)
        return pltpu.make_async_copy(acc_ref.at[:, cols], out_hbm.at[pl.ds(row0, tm), cols],
                                     sem.at[c])

    @pl.when(j == 0)
    def _():
        for c in range(n_c):
            row_copy(c, load=True).start()
        for c in range(n_c):
            row_copy(c, load=True).wait()
            _rms_rows(acc_ref.at[_chunk_rows(c), :], g_ref, xn_ref.at[_chunk_rows(c), :])

    def step(last):
        for src, dst in zip(side_in, side_out):
            dst[...] = src[...].astype(BF16)
        xn = xn_ref[...]
        gate = jnp.dot(xn, wg_ref[...].astype(BF16), preferred_element_type=F32)
        up = jnp.dot(xn, wu_ref[...].astype(BF16), preferred_element_type=F32)
        act = (0.5 * _silu(gate) * up).astype(BF16)
        wo = wo_ref[...].astype(BF16)
        if not last:
            acc_ref[...] += jnp.dot(act, wo, preferred_element_type=F32)
        elif final:
            acc_ref[...] += jnp.dot(act, wo, preferred_element_type=F32)
            for c in range(n_c):
                rows = acc_ref.at[_chunk_rows(c), :]
                _rms_rows(rows, fg_ref, rows)
                row_copy(c, load=False).start()
            for c in range(n_c):
                row_copy(c, load=False).wait()
        else:
            dc = d // OUT_COL_CHUNKS
            for c in range(OUT_COL_CHUNKS):
                cols = slice(c * dc, (c + 1) * dc)
                acc_ref[:, cols] += jnp.dot(act, wo[:, cols], preferred_element_type=F32)
                col_copy(c).start()
            for c in range(OUT_COL_CHUNKS):
                col_copy(c).wait()

    @pl.when(j < n_j - 1)
    def _():
        step(False)

    @pl.when(j == n_j - 1)
    def _():
        step(True)


def _ffn(h, g, w_in, w_out, final_g=None, side_weights=(), *, tm=1024, tf=256):
    m, d = h.shape
    d_ff = w_out.shape[0]
    n_j = d_ff // tf
    n_steps = (m // tm) * n_j
    assert d_ff % tf == 0 and m % tm == 0 and tm % ROW_CHUNK == 0
    assert d % (OUT_COL_CHUNKS * LANES) == 0 and OUT_COL_CHUNKS <= tm // ROW_CHUNK
    final = final_g is not None
    fg = final_g if final else g
    vmem = tm * d * 4 + tm * d * 2 + 2 * 3 * d * tf * 4

    side_specs = []
    for w in side_weights:
        rows = w.shape[0] // SIDE_BLOCKS
        assert w.shape[0] % SIDE_BLOCKS == 0 and rows % BF16_SUBLANES == 0
        assert SIDE_BLOCKS <= n_steps
        vmem += 2 * rows * w.shape[1] * (4 + 2)
        side_specs.append(pl.BlockSpec(
            (rows, w.shape[1]), lambda i, j: (jnp.minimum(i * n_j + j, SIDE_BLOCKS - 1), 0)))

    return pl.pallas_call(
        functools.partial(_ffn_body, n_j=n_j, final=final, n_side=len(side_weights)),
        grid=(m // tm, n_j),
        in_specs=[
            pl.BlockSpec(memory_space=pl.ANY),
            pl.BlockSpec((1, d), lambda i, j: (0, 0)),
            pl.BlockSpec((d, tf), lambda i, j: (0, j)),
            pl.BlockSpec((d, tf), lambda i, j: (0, j + n_j)),
            pl.BlockSpec((tf, d), lambda i, j: (j, 0)),
            pl.BlockSpec((1, d), lambda i, j: (0, 0)),
        ] + side_specs,
        out_specs=[pl.BlockSpec(memory_space=pl.ANY)] + side_specs,
        out_shape=[jax.ShapeDtypeStruct((m, d), F32)]
        + [jax.ShapeDtypeStruct(w.shape, BF16) for w in side_weights],
        scratch_shapes=[pltpu.VMEM((tm, d), F32), pltpu.VMEM((tm, d), BF16),
                        pltpu.SemaphoreType.DMA((tm // ROW_CHUNK,))],
        compiler_params=_params(vmem),
        name="ffn",
    )(h, g.reshape(1, d), w_in, w_in, w_out, fg.reshape(1, d), *side_weights)


def _conv_in_body(h_hbm, g_ref, wb_ref, wc_ref, wh_ref, cw_ref, y_ref,
                  xn_ref, stage_ref, sem, carry_ref, *, tiles_per_seq):
    i = pl.program_id(0)
    j = pl.program_id(1)

    @pl.when(j == 0)
    def _():
        _stream_rms(h_hbm, g_ref, xn_ref, stage_ref, sem)

    xn = xn_ref[...]
    b = jnp.dot(xn, wb_ref[...], preferred_element_type=F32)
    c = jnp.dot(xn, wc_ref[...], preferred_element_type=F32)
    hh = jnp.dot(xn, wh_ref[...], preferred_element_type=F32)
    u = c * hh
    tm = u.shape[0]

    @pl.when(i % tiles_per_seq == 0)
    def _():
        carry_ref[j] = jnp.zeros(carry_ref.shape[1:], F32)

    prev = carry_ref[j]
    carry_ref[j] = u[tm - 8:, :]
    row = lax.broadcasted_iota(jnp.int32, u.shape, 0)
    u1 = jnp.where(row < 1, prev[7:8, :], pltpu.roll(u, 1, 0))
    u2 = jnp.where(row < 2, jnp.where(row < 1, prev[6:7, :], prev[7:8, :]), pltpu.roll(u, 2, 0))
    cw = cw_ref[...]
    conv = cw[0:1, :] * u2 + cw[1:2, :] * u1 + cw[2:3, :] * u
    y_ref[...] = (b * conv).astype(BF16)


def _conv_in(h, g, w_in, conv_w, *, seq, tm=1024, tc=512):
    m, d = h.shape
    n_j = d // tc
    assert m % tm == 0 and seq % tm == 0 and d % tc == 0
    vmem = (2 * ROW_CHUNK * d * 4 + tm * d * 2 + 2 * 3 * d * tc * 2 + 2 * tm * tc * 2
            + n_j * 8 * tc * 4 + 6 * tm * tc * 4)
    return pl.pallas_call(
        functools.partial(_conv_in_body, tiles_per_seq=seq // tm),
        grid=(m // tm, n_j),
        in_specs=[
            pl.BlockSpec(memory_space=pl.ANY),
            pl.BlockSpec((1, d), lambda i, j: (0, 0)),
            pl.BlockSpec((d, tc), lambda i, j: (0, j)),
            pl.BlockSpec((d, tc), lambda i, j: (0, j + n_j)),
            pl.BlockSpec((d, tc), lambda i, j: (0, j + 2 * n_j)),
            pl.BlockSpec((CONV_WIDTH, tc), lambda i, j: (0, j)),
        ],
        out_specs=pl.BlockSpec((tm, tc), lambda i, j: (i, j)),
        out_shape=jax.ShapeDtypeStruct((m, d), BF16),
        scratch_shapes=_stream_scratch(tm, d) + [pltpu.VMEM((n_j, 8, tc), F32)],
        compiler_params=_params(vmem),
        name="conv_in",
    )(h, g.reshape(1, d), w_in, w_in, w_in, conv_w)


def _proj_body(h_hbm, g_ref, w_ref, *rest, n_q_tiles):
    *rope_refs, out_ref, xn_ref, stage_ref, sem = rest
    j = pl.program_id(1)

    @pl.when(j == 0)
    def _():
        _stream_rms(h_hbm, g_ref, xn_ref, stage_ref, sem)

    t = jnp.dot(xn_ref[...], w_ref[...].astype(BF16), preferred_element_type=F32)
    if not rope_refs:
        out_ref[...] = t.astype(BF16)
        return
    cos_ref, sin_ref = rope_refs
    cos = cos_ref[...]
    sin = sin_ref[...]
    scale = jnp.where(j < n_q_tiles, RET_QK_DIM ** -0.5, 1.0)
    half = RET_QK_DIM // 2
    for hd in range(t.shape[1] // RET_QK_DIM):
        lo = hd * RET_QK_DIM
        t1 = t[:, lo:lo + half]
        t2 = t[:, lo + half:lo + 2 * half]
        out_ref[:, lo:lo + half] = ((t1 * cos - t2 * sin) * scale).astype(BF16)
        out_ref[:, lo + half:lo + 2 * half] = ((t1 * sin + t2 * cos) * scale).astype(BF16)


def _proj(h, g, w, rope_tables=(), *, col0, n, seq, tm=1024, tn=1024):
    m, d = h.shape
    assert m % tm == 0 and n % tn == 0 and col0 % tn == 0 and seq % tm == 0
    half = RET_QK_DIM // 2
    assert half == LANES and tn % RET_QK_DIM == 0
    j0 = col0 // tn
    tiles_per_seq = seq // tm
    vmem = (2 * ROW_CHUNK * d * 4 + tm * d * 2 + 2 * d * tn * 4 + 2 * tm * tn * 2
            + 4 * tm * half * 4)
    table_spec = pl.BlockSpec((tm, half), lambda i, j: (i % tiles_per_seq, 0))
    return pl.pallas_call(
        functools.partial(_proj_body, n_q_tiles=D_MODEL // tn),
        grid=(m // tm, n // tn),
        in_specs=[
            pl.BlockSpec(memory_space=pl.ANY),
            pl.BlockSpec((1, d), lambda i, j: (0, 0)),
            pl.BlockSpec((d, tn), lambda i, j: (0, j + j0)),
        ] + [table_spec] * len(rope_tables),
        out_specs=pl.BlockSpec((tm, tn), lambda i, j: (i, j)),
        out_shape=jax.ShapeDtypeStruct((m, n), BF16),
        scratch_shapes=_stream_scratch(tm, d),
        compiler_params=_params(vmem),
        name="ret_proj_rope" if rope_tables else "ret_proj",
    )(h, g.reshape(1, d), w, *rope_tables)


def _ret_body(lg_ref, q_ref, k_ref, v_ref, gate_ref, gn_ref, out_ref,
              state_ref, dmat_ref, qdec_ref, kdec_ref, *, t_blk, heads):
    hg = pl.program_id(1)
    t = pl.program_id(2)

    @pl.when(t == 0)
    def _():
        state_ref[...] = jnp.zeros_like(state_ref)
        row = lax.broadcasted_iota(jnp.int32, (t_blk, t_blk), 0)
        col = lax.broadcasted_iota(jnp.int32, (t_blk, t_blk), 1)
        shift = CHUNK.bit_length() - 1
        rc = lax.shift_right_logical(row, shift)
        cc = lax.shift_right_logical(col, shift)
        diff = row - col
        visible = cc <= rc
        dist = jnp.where(visible, jnp.where(rc == cc, jnp.abs(diff), diff), 0).astype(F32)
        idx = lax.broadcasted_iota(jnp.int32, (t_blk, RET_QK_DIM), 0).astype(F32)
        for hh in range(heads):
            lg = lg_ref[hg * heads + hh]
            dmat_ref[hh] = jnp.where(visible, jnp.exp(lg * dist), 0.0)
            qdec_ref[hh] = jnp.exp(lg * (idx + 1.0))
            kdec_ref[hh] = jnp.exp(lg * (t_blk - 1.0 - idx))

    for hh in range(heads):
        qk_cols = slice(hh * RET_QK_DIM, (hh + 1) * RET_QK_DIM)
        v_cols = slice(hh * RET_V_DIM, (hh + 1) * RET_V_DIM)
        q = q_ref[:, qk_cols]
        k = k_ref[:, qk_cols]
        v = v_ref[:, v_cols]
        s = lax.dot_general(q, k, (((1,), (1,)), ((), ())), preferred_element_type=F32)
        p = (s * dmat_ref[hh]).astype(BF16)
        o = jnp.dot(p, v, preferred_element_type=F32)

        qx = (q.astype(F32) * qdec_ref[hh]).astype(BF16)
        kxt = (k.astype(F32) * kdec_ref[hh]).T.astype(BF16)
        st = state_ref[hh]
        o = o + jnp.dot(qx, st.astype(BF16), preferred_element_type=F32)
        blk_dec = qdec_ref[hh, t_blk - 1:t_blk, 0:1]
        state_ref[hh] = blk_dec * st + jnp.dot(kxt, v, preferred_element_type=F32)

        mu = jnp.mean(o, axis=-1, keepdims=True)
        dlt = o - mu
        var = jnp.mean(dlt * dlt, axis=-1, keepdims=True)
        y = (dlt * lax.rsqrt(var + GN_EPS)) * gn_ref[:, v_cols]
        out_ref[:, v_cols] = (_silu(gate_ref[:, v_cols].astype(F32)) * y).astype(BF16)


def _retention(qk, vg, gn_g, log_gamma, *, bsz, seq, t_blk=256, heads=4):
    m = qk.shape[0]
    nt = seq // t_blk
    n_hg = RET_HEADS // heads
    assert seq % t_blk == 0 and t_blk % CHUNK == 0 and RET_HEADS % heads == 0
    qk_w = heads * RET_QK_DIM
    v_w = heads * RET_V_DIM
    return pl.pallas_call(
        functools.partial(_ret_body, t_blk=t_blk, heads=heads),
        grid=(bsz, n_hg, nt),
        in_specs=[
            pl.BlockSpec(memory_space=pltpu.SMEM),
            pl.BlockSpec((t_blk, qk_w), lambda b, h, t: (b * nt + t, h)),
            pl.BlockSpec((t_blk, qk_w), lambda b, h, t: (b * nt + t, h + n_hg)),
            pl.BlockSpec((t_blk, v_w), lambda b, h, t: (b * nt + t, h)),
            pl.BlockSpec((t_blk, v_w), lambda b, h, t: (b * nt + t, h + n_hg)),
            pl.BlockSpec((1, v_w), lambda b, h, t: (0, h)),
        ],
        out_specs=pl.BlockSpec((t_blk, v_w), lambda b, h, t: (b * nt + t, h)),
        out_shape=jax.ShapeDtypeStruct((m, RET_HEADS * RET_V_DIM), BF16),
        scratch_shapes=[pltpu.VMEM((heads, RET_QK_DIM, RET_V_DIM), F32),
                        pltpu.VMEM((heads, t_blk, t_blk), F32),
                        pltpu.VMEM((heads, t_blk, RET_QK_DIM), F32),
                        pltpu.VMEM((heads, t_blk, RET_QK_DIM), F32)],
        compiler_params=pltpu.CompilerParams(
            dimension_semantics=("arbitrary", "arbitrary", "arbitrary")),
        name="retention",
    )(log_gamma, qk, qk, vg, vg, gn_g.reshape(1, -1))


def _out_proj_body(y_ref, w_ref, h_ref, out_ref):
    w = w_ref[...]
    if w.dtype != BF16:
        w = w.astype(BF16)
    out_ref[...] = h_ref[...] + jnp.dot(y_ref[...], w, preferred_element_type=F32)


def _out_proj(y, w, h, *, tm, tn, name):
    m, kdim = y.shape
    d = w.shape[1]
    assert m % tm == 0 and d % tn == 0
    vmem = 2 * tm * kdim * 2 + 2 * kdim * tn * w.dtype.itemsize + 4 * tm * tn * 4
    return pl.pallas_call(
        _out_proj_body,
        grid=(m // tm, d // tn),
        in_specs=[
            pl.BlockSpec((tm, kdim), lambda i, j: (i, 0)),
            pl.BlockSpec((kdim, tn), lambda i, j: (0, j)),
            pl.BlockSpec((tm, tn), lambda i, j: (i, j)),
        ],
        out_specs=pl.BlockSpec((tm, tn), lambda i, j: (i, j)),
        out_shape=jax.ShapeDtypeStruct((m, d), F32),
        compiler_params=_params(vmem),
        name=name,
    )(y, w, h)


def kernel(x, l0_norm_ffn1, l0_ffn1_w_in, l0_ffn1_w_out, l0_norm_mix, l0_conv_w_in, l0_conv_w, l0_conv_w_out, l0_norm_ffn2, l0_ffn2_w_in, l0_ffn2_w_out, l1_norm_ffn1, l1_ffn1_w_in, l1_ffn1_w_out, l1_norm_mix, l1_ret_w_in, l1_ret_gn, l1_ret_w_out, l1_norm_ffn2, l1_ffn2_w_in, l1_ffn2_w_out, final_norm):
    bsz, seq, d = x.shape
    h = x.reshape(bsz * seq, d)

    h, conv_w_in = _ffn(h, l0_norm_ffn1, l0_ffn1_w_in, l0_ffn1_w_out,
                        side_weights=(l0_conv_w_in,))
    y = _conv_in(h, l0_norm_mix, conv_w_in, l0_conv_w, seq=seq)
    h = _out_proj(y, l0_conv_w_out, h, tm=1024, tn=512, name="conv_out_proj")
    h, ret_w_out = _ffn(h, l0_norm_ffn2, l0_ffn2_w_in, l0_ffn2_w_out,
                        side_weights=(l1_ret_w_out,))

    h, = _ffn(h, l1_norm_ffn1, l1_ffn1_w_in, l1_ffn1_w_out)

    half = RET_QK_DIM // 2
    inv = ROPE_BASE ** (-jnp.arange(half, dtype=F32) / half)
    ang = jnp.arange(seq, dtype=jnp.int32).astype(F32)[:, None] * inv[None, :]
    cos, sin = jnp.cos(ang), jnp.sin(ang)
    log_gamma = jnp.log1p(-jnp.exp2(-5.0 - jnp.arange(RET_HEADS, dtype=F32)))

    qk = _proj(h, l1_norm_mix, l1_ret_w_in, (cos, sin), col0=0, n=2 * d, seq=seq, tn=512)
    vg = _proj(h, l1_norm_mix, l1_ret_w_in, col0=2 * d, n=4 * d, seq=seq)
    y = _retention(qk, vg, l1_ret_gn, log_gamma, bsz=bsz, seq=seq)
    h = _out_proj(y, ret_w_out, h, tm=512, tn=512, name="ret_out_proj")

    h, = _ffn(h, l1_norm_ffn2, l1_ffn2_w_in, l1_ffn2_w_out, final_norm)
    return h.reshape(bsz, seq, d)
```

```python
import functools

import jax
import jax.numpy as jnp
from jax import lax
from jax.experimental import pallas as pl
from jax.experimental.pallas import tpu as pltpu

D_MODEL = 4096
D_FF = 11008
CHUNK = 64
CONV_WIDTH = 3
RET_HEADS = 16
RET_QK_DIM = D_MODEL // RET_HEADS
RET_V_DIM = 2 * D_MODEL // RET_HEADS
ROPE_BASE = 10000.0
EPS = 1e-6
GN_EPS = 1e-5

F32 = jnp.float32
BF16 = jnp.bfloat16

V7X_VMEM_LIMIT_BYTES = 60000 * 1024
LANES = 128


def _params(vmem_bytes):
    assert vmem_bytes <= V7X_VMEM_LIMIT_BYTES, vmem_bytes
    return pltpu.CompilerParams(
        dimension_semantics=("arbitrary", "arbitrary"),
        vmem_limit_bytes=V7X_VMEM_LIMIT_BYTES,
    )


NORM_ROWS = 16
NORM_UNROLL = 4
ROW_CHUNK = 128


def _rms(x, g):
    ms = jnp.mean(x * x, axis=-1, keepdims=True)
    return (x * lax.rsqrt(ms + EPS)) * g


def _rms_rows(src_ref, g_ref, dst_ref):
    g = g_ref[...]
    rows_per_step = NORM_ROWS * NORM_UNROLL
    assert src_ref.shape[0] % rows_per_step == 0

    def step(r, carry):
        base = pl.multiple_of(r * rows_per_step, rows_per_step)
        slabs = [pl.ds(base + u * NORM_ROWS, NORM_ROWS) for u in range(NORM_UNROLL)]
        xs = [src_ref[rows, :] for rows in slabs]
        for rows, x in zip(slabs, xs):
            dst_ref[rows, :] = _rms(x, g).astype(dst_ref.dtype)
        return carry

    lax.fori_loop(0, src_ref.shape[0] // rows_per_step, step, 0)


def _chunk_rows(c):
    return pl.ds(c * ROW_CHUNK, ROW_CHUNK)


def _tile_row0(tm):
    return pl.multiple_of(pl.program_id(0) * tm, tm)


def _stream_rms(h_hbm, g_ref, xn_ref, stage_ref, sem):
    tm = xn_ref.shape[0]
    chunk = stage_ref.shape[1]
    n_c = tm // chunk
    row0 = _tile_row0(tm)

    def copy(c):
        src = h_hbm.at[pl.ds(row0 + c * chunk, chunk), :]
        return pltpu.make_async_copy(src, stage_ref.at[c % 2], sem.at[c % 2])

    copy(0).start()
    for c in range(n_c):
        if c + 1 < n_c:
            copy(c + 1).start()
        copy(c).wait()
        _rms_rows(stage_ref.at[c % 2], g_ref, xn_ref.at[pl.ds(c * chunk, chunk), :])


def _stream_scratch(tm, d, chunk=ROW_CHUNK):
    assert tm % chunk == 0 and chunk % (NORM_ROWS * NORM_UNROLL) == 0
    return [pltpu.VMEM((tm, d), BF16), pltpu.VMEM((2, chunk, d), F32),
            pltpu.SemaphoreType.DMA((2,))]


def _silu(x):
    return x * jax.nn.sigmoid(x)


OUT_COL_CHUNKS = 4
SIDE_BLOCKS = 256
BF16_SUBLANES = 16


def _ffn_body(h_hbm, g_ref, wg_ref, wu_ref, wo_ref, fg_ref, *rest, n_j, final, n_side):
    side_in = rest[:n_side]
    out_hbm = rest[n_side]
    side_out = rest[n_side + 1:2 * n_side + 1]
    acc_ref, xn_ref, sem = rest[2 * n_side + 1:]
    j = pl.program_id(1)
    tm, d = acc_ref.shape
    n_c = tm // ROW_CHUNK
    row0 = _tile_row0(tm)

    def row_copy(c, *, load):
        hbm_rows = pl.ds(row0 + c * ROW_CHUNK, ROW_CHUNK)
        vm = acc_ref.at[_chunk_rows(c), :]
        if load:
            return pltpu.make_async_copy(h_hbm.at[hbm_rows, :], vm, sem.at[c])
        return pltpu.make_async_copy(vm, out_hbm.at[hbm_rows, :], sem.at[c])

    def col_copy(c):
        cols = pl.ds(c * (d // OUT_COL_CHUNKS), d // OUT_COL_CHUNKS)
        return pltpu.make_async_copy(acc_ref.at[:, cols], out_hbm.at[pl.ds(row0, tm), cols],
                                     sem.at[c])

    @pl.when(j == 0)
    def _():
        for c in range(n_c):
            row_copy(c, load=True).start()
        for c in range(n_c):
            row_copy(c, load=True).wait()
            _rms_rows(acc_ref.at[_chunk_rows(c), :], g_ref, xn_ref.at[_chunk_rows(c), :])

    def step(last):
        for src, dst in zip(side_in, side_out):
            dst[...] = src[...].astype(BF16)
        xn = xn_ref[...]
        gate = jnp.dot(xn, wg_ref[...].astype(BF16), preferred_element_type=F32)
        up = jnp.dot(xn, wu_ref[...].astype(BF16), preferred_element_type=F32)
        act = (0.5 * _silu(gate) * up).astype(BF16)
        wo = wo_ref[...].astype(BF16)
        if not last:
            acc_ref[...] += jnp.dot(act, wo, preferred_element_type=F32)
        elif final:
            acc_ref[...] += jnp.dot(act, wo, preferred_element_type=F32)
            for c in range(n_c):
                rows = acc_ref.at[_chunk_rows(c), :]
                _rms_rows(rows, fg_ref, rows)
                row_copy(c, load=False).start()
            for c in range(n_c):
                row_copy(c, load=False).wait()
        else:
            dc = d // OUT_COL_CHUNKS
            for c in range(OUT_COL_CHUNKS):
                cols = slice(c * dc, (c + 1) * dc)
                acc_ref[:, cols] += jnp.dot(act, wo[:, cols], preferred_element_type=F32)
                col_copy(c).start()
            for c in range(OUT_COL_CHUNKS):
                col_copy(c).wait()

    @pl.when(j < n_j - 1)
    def _():
        step(False)

    @pl.when(j == n_j - 1)
    def _():
        step(True)


def _ffn(h, g, w_in, w_out, final_g=None, side_weights=(), *, tm=1024, tf=256):
    m, d = h.shape
    d_ff = w_out.shape[0]
    n_j = d_ff // tf
    n_steps = (m // tm) * n_j
    assert d_ff % tf == 0 and m % tm == 0 and tm % ROW_CHUNK == 0
    assert d % (OUT_COL_CHUNKS * LANES) == 0 and OUT_COL_CHUNKS <= tm // ROW_CHUNK
    final = final_g is not None
    fg = final_g if final else g
    vmem = tm * d * 4 + tm * d * 2 + 2 * 3 * d * tf * 4

    side_specs = []
    for w in side_weights:
        rows = w.shape[0] // SIDE_BLOCKS
        assert w.shape[0] % SIDE_BLOCKS == 0 and rows % BF16_SUBLANES == 0
        assert SIDE_BLOCKS <= n_steps
        vmem += 2 * rows * w.shape[1] * (4 + 2)
        side_specs.append(pl.BlockSpec(
            (rows, w.shape[1]), lambda i, j: (jnp.minimum(i * n_j + j, SIDE_BLOCKS - 1), 0)))

    return pl.pallas_call(
        functools.partial(_ffn_body, n_j=n_j, final=final, n_side=len(side_weights)),
        grid=(m // tm, n_j),
        in_specs=[
            pl.BlockSpec(memory_space=pl.ANY),
            pl.BlockSpec((1, d), lambda i, j: (0, 0)),
            pl.BlockSpec((d, tf), lambda i, j: (0, j)),
            pl.BlockSpec((d, tf), lambda i, j: (0, j + n_j)),
            pl.BlockSpec((tf, d), lambda i, j: (j, 0)),
            pl.BlockSpec((1, d), lambda i, j: (0, 0)),
        ] + side_specs,
        out_specs=[pl.BlockSpec(memory_space=pl.ANY)] + side_specs,
        out_shape=[jax.ShapeDtypeStruct((m, d), F32)]
        + [jax.ShapeDtypeStruct(w.shape, BF16) for w in side_weights],
        scratch_shapes=[pltpu.VMEM((tm, d), F32), pltpu.VMEM((tm, d), BF16),
                        pltpu.SemaphoreType.DMA((tm // ROW_CHUNK,))],
        compiler_params=_params(vmem),
        name="ffn",
    )(h, g.reshape(1, d), w_in, w_in, w_out, fg.reshape(1, d), *side_weights)


def _conv_in_body(h_hbm, g_ref, wb_ref, wc_ref, wh_ref, cw_ref, y_ref,
                  xn_ref, stage_ref, sem, carry_ref, *, tiles_per_seq):
    i = pl.program_id(0)
    j = pl.program_id(1)

    @pl.when(j == 0)
    def _():
        _stream_rms(h_hbm, g_ref, xn_ref, stage_ref, sem)

    xn = xn_ref[...]
    b = jnp.dot(xn, wb_ref[...], preferred_element_type=F32)
    c = jnp.dot(xn, wc_ref[...], preferred_element_type=F32)
    hh = jnp.dot(xn, wh_ref[...], preferred_element_type=F32)
    u = c * hh
    tm = u.shape[0]

    @pl.when(i % tiles_per_seq == 0)
    def _():
        carry_ref[j] = jnp.zeros(carry_ref.shape[1:], F32)

    prev = carry_ref[j]
    carry_ref[j] = u[tm - 8:, :]
    row = lax.broadcasted_iota(jnp.int32, u.shape, 0)
    u1 = jnp.where(row < 1, prev[7:8, :], pltpu.roll(u, 1, 0))
    u2 = jnp.where(row < 2, jnp.where(row < 1, prev[6:7, :], prev[7:8, :]), pltpu.roll(u, 2, 0))
    cw = cw_ref[...]
    conv = cw[0:1, :] * u2 + cw[1:2, :] * u1 + cw[2:3, :] * u
    y_ref[...] = (b * conv).astype(BF16)


def _conv_in(h, g, w_in, conv_w, *, seq, tm=1024, tc=512):
    m, d = h.shape
    n_j = d // tc
    assert m % tm == 0 and seq % tm == 0 and d % tc == 0
    vmem = (2 * ROW_CHUNK * d * 4 + tm * d * 2 + 2 * 3 * d * tc * 2 + 2 * tm * tc * 2
            + n_j * 8 * tc * 4 + 6 * tm * tc * 4)
    return pl.pallas_call(
        functools.partial(_conv_in_body, tiles_per_seq=seq // tm),
        grid=(m // tm, n_j),
        in_specs=[
            pl.BlockSpec(memory_space=pl.ANY),
            pl.BlockSpec((1, d), lambda i, j: (0, 0)),
            pl.BlockSpec((d, tc), lambda i, j: (0, j)),
            pl.BlockSpec((d, tc), lambda i, j: (0, j + n_j)),
            pl.BlockSpec((d, tc), lambda i, j: (0, j + 2 * n_j)),
            pl.BlockSpec((CONV_WIDTH, tc), lambda i, j: (0, j)),
        ],
        out_specs=pl.BlockSpec((tm, tc), lambda i, j: (i, j)),
        out_shape=jax.ShapeDtypeStruct((m, d), BF16),
        scratch_shapes=_stream_scratch(tm, d) + [pltpu.VMEM((n_j, 8, tc), F32)],
        compiler_params=_params(vmem),
        name="conv_in",
    )(h, g.reshape(1, d), w_in, w_in, w_in, conv_w)


PROJ_ROW_CHUNK = 64


def _proj_body(h_hbm, g_ref, w_ref, cos_ref, sin_ref, out_ref, xn_ref, stage_ref, sem,
               *, n_q_tiles):
    j = pl.program_id(1)

    @pl.when(j == 0)
    def _():
        _stream_rms(h_hbm, g_ref, xn_ref, stage_ref, sem)

    def project():
        return jnp.dot(xn_ref[...], w_ref[...].astype(BF16), preferred_element_type=F32)

    @pl.when(j < 2 * n_q_tiles)
    def _():
        t = project()
        cos = cos_ref[...]
        sin = sin_ref[...]
        scale = jnp.where(j < n_q_tiles, RET_QK_DIM ** -0.5, 1.0)
        half = RET_QK_DIM // 2
        for hd in range(t.shape[1] // RET_QK_DIM):
            lo = hd * RET_QK_DIM
            t1 = t[:, lo:lo + half]
            t2 = t[:, lo + half:lo + 2 * half]
            out_ref[:, lo:lo + half] = ((t1 * cos - t2 * sin) * scale).astype(BF16)
            out_ref[:, lo + half:lo + 2 * half] = ((t1 * sin + t2 * cos) * scale).astype(BF16)

    @pl.when(j >= 2 * n_q_tiles)
    def _():
        out_ref[...] = project().astype(BF16)


def _proj(h, g, w, cos, sin, *, seq, tm=1024, tn=1024):
    m, d = h.shape
    n = w.shape[1]
    assert m % tm == 0 and n % tn == 0 and d % tn == 0 and seq % tm == 0
    half = RET_QK_DIM // 2
    assert half == LANES and tn % RET_QK_DIM == 0
    tiles_per_seq = seq // tm
    vmem = (2 * PROJ_ROW_CHUNK * d * 4 + tm * d * 2 + 2 * d * tn * 4 + 2 * tm * tn * 2
            + 4 * tm * half * 4)
    table_spec = pl.BlockSpec((tm, half), lambda i, j: (i % tiles_per_seq, 0))
    return pl.pallas_call(
        functools.partial(_proj_body, n_q_tiles=d // tn),
        grid=(m // tm, n // tn),
        in_specs=[
            pl.BlockSpec(memory_space=pl.ANY),
            pl.BlockSpec((1, d), lambda i, j: (0, 0)),
            pl.BlockSpec((d, tn), lambda i, j: (0, j)),
            table_spec,
            table_spec,
        ],
        out_specs=pl.BlockSpec((tm, tn), lambda i, j: (i, j)),
        out_shape=jax.ShapeDtypeStruct((m, n), BF16),
        scratch_shapes=_stream_scratch(tm, d, PROJ_ROW_CHUNK),
        compiler_params=_params(vmem),
        name="ret_proj",
    )(h, g.reshape(1, d), w, cos, sin)


def _ret_body(lg_ref, q_ref, k_ref, v_ref, gate_ref, gn_ref, out_ref,
              state_ref, dmat_ref, qdec_ref, kdec_ref, *, t_blk, heads):
    hg = pl.program_id(1)
    t = pl.program_id(2)

    @pl.when(t == 0)
    def _():
        state_ref[...] = jnp.zeros_like(state_ref)
        row = lax.broadcasted_iota(jnp.int32, (t_blk, t_blk), 0)
        col = lax.broadcasted_iota(jnp.int32, (t_blk, t_blk), 1)
        shift = CHUNK.bit_length() - 1
        rc = lax.shift_right_logical(row, shift)
        cc = lax.shift_right_logical(col, shift)
        diff = row - col
        visible = cc <= rc
        dist = jnp.where(visible, jnp.where(rc == cc, jnp.abs(diff), diff), 0).astype(F32)
        idx = lax.broadcasted_iota(jnp.int32, (t_blk, RET_QK_DIM), 0).astype(F32)
        for hh in range(heads):
            lg = lg_ref[hg * heads + hh]
            dmat_ref[hh] = jnp.where(visible, jnp.exp(lg * dist), 0.0)
            qdec_ref[hh] = jnp.exp(lg * (idx + 1.0))
            kdec_ref[hh] = jnp.exp(lg * (t_blk - 1.0 - idx))

    for hh in range(heads):
        qk_cols = slice(hh * RET_QK_DIM, (hh + 1) * RET_QK_DIM)
        v_cols = slice(hh * RET_V_DIM, (hh + 1) * RET_V_DIM)
        q = q_ref[:, qk_cols]
        k = k_ref[:, qk_cols]
        v = v_ref[:, v_cols]
        s = lax.dot_general(q, k, (((1,), (1,)), ((), ())), preferred_element_type=F32)
        p = (s * dmat_ref[hh]).astype(BF16)
        o = jnp.dot(p, v, preferred_element_type=F32)

        qx = (q.astype(F32) * qdec_ref[hh]).astype(BF16)
        kxt = (k.astype(F32) * kdec_ref[hh]).T.astype(BF16)
        st = state_ref[hh]
        o = o + jnp.dot(qx, st.astype(BF16), preferred_element_type=F32)
        blk_dec = qdec_ref[hh, t_blk - 1:t_blk, 0:1]
        state_ref[hh] = blk_dec * st + jnp.dot(kxt, v, preferred_element_type=F32)

        mu = jnp.mean(o, axis=-1, keepdims=True)
        dlt = o - mu
        var = jnp.mean(dlt * dlt, axis=-1, keepdims=True)
        y = (dlt * lax.rsqrt(var + GN_EPS)) * gn_ref[:, v_cols]
        out_ref[:, v_cols] = (_silu(gate_ref[:, v_cols].astype(F32)) * y).astype(BF16)


def _retention(qkvg, gn_g, log_gamma, *, bsz, seq, t_blk=256, heads=8):
    m = qkvg.shape[0]
    nt = seq // t_blk
    n_hg = RET_HEADS // heads
    assert seq % t_blk == 0 and t_blk % CHUNK == 0 and RET_HEADS % heads == 0
    qk_w = heads * RET_QK_DIM
    v_w = heads * RET_V_DIM
    return pl.pallas_call(
        functools.partial(_ret_body, t_blk=t_blk, heads=heads),
        grid=(bsz, n_hg, nt),
        in_specs=[
            pl.BlockSpec(memory_space=pltpu.SMEM),
            pl.BlockSpec((t_blk, qk_w), lambda b, h, t: (b * nt + t, h)),
            pl.BlockSpec((t_blk, qk_w), lambda b, h, t: (b * nt + t, h + n_hg)),
            pl.BlockSpec((t_blk, v_w), lambda b, h, t: (b * nt + t, h + n_hg)),
            pl.BlockSpec((t_blk, v_w), lambda b, h, t: (b * nt + t, h + 2 * n_hg)),
            pl.BlockSpec((1, v_w), lambda b, h, t: (0, h)),
        ],
        out_specs=pl.BlockSpec((t_blk, v_w), lambda b, h, t: (b * nt + t, h)),
        out_shape=jax.ShapeDtypeStruct((m, RET_HEADS * RET_V_DIM), BF16),
        scratch_shapes=[pltpu.VMEM((heads, RET_QK_DIM, RET_V_DIM), F32),
                        pltpu.VMEM((heads, t_blk, t_blk), F32),
                        pltpu.VMEM((heads, t_blk, RET_QK_DIM), F32),
                        pltpu.VMEM((heads, t_blk, RET_QK_DIM), F32)],
        compiler_params=pltpu.CompilerParams(
            dimension_semantics=("arbitrary", "arbitrary", "arbitrary")),
        name="retention",
    )(log_gamma, qkvg, qkvg, qkvg, qkvg, gn_g.reshape(1, -1))


def _out_proj_body(y_ref, w_ref, h_ref, out_ref):
    w = w_ref[...]
    if w.dtype != BF16:
        w = w.astype(BF16)
    out_ref[...] = h_ref[...] + jnp.dot(y_ref[...], w, preferred_element_type=F32)


def _out_proj(y, w, h, *, tm, tn, name):
    m, kdim = y.shape
    d = w.shape[1]
    assert m % tm == 0 and d % tn == 0
    vmem = 2 * tm * kdim * 2 + 2 * kdim * tn * w.dtype.itemsize + 4 * tm * tn * 4
    return pl.pallas_call(
        _out_proj_body,
        grid=(m // tm, d // tn),
        in_specs=[
            pl.BlockSpec((tm, kdim), lambda i, j: (i, 0)),
            pl.BlockSpec((kdim, tn), lambda i, j: (0, j)),
            pl.BlockSpec((tm, tn), lambda i, j: (i, j)),
        ],
        out_specs=pl.BlockSpec((tm, tn), lambda i, j: (i, j)),
        out_shape=jax.ShapeDtypeStruct((m, d), F32),
        compiler_params=_params(vmem),
        name=name,
    )(y, w, h)


def kernel(x, l0_norm_ffn1, l0_ffn1_w_in, l0_ffn1_w_out, l0_norm_mix, l0_conv_w_in, l0_conv_w, l0_conv_w_out, l0_norm_ffn2, l0_ffn2_w_in, l0_ffn2_w_out, l1_norm_ffn1, l1_ffn1_w_in, l1_ffn1_w_out, l1_norm_mix, l1_ret_w_in, l1_ret_gn, l1_ret_w_out, l1_norm_ffn2, l1_ffn2_w_in, l1_ffn2_w_out, final_norm):
    bsz, seq, d = x.shape
    h = x.reshape(bsz * seq, d)

    h, conv_w_in = _ffn(h, l0_norm_ffn1, l0_ffn1_w_in, l0_ffn1_w_out,
                        side_weights=(l0_conv_w_in,))
    y = _conv_in(h, l0_norm_mix, conv_w_in, l0_conv_w, seq=seq)
    h = _out_proj(y, l0_conv_w_out, h, tm=1024, tn=512, name="conv_out_proj")
    h, ret_w_out = _ffn(h, l0_norm_ffn2, l0_ffn2_w_in, l0_ffn2_w_out,
                        side_weights=(l1_ret_w_out,))

    h, = _ffn(h, l1_norm_ffn1, l1_ffn1_w_in, l1_ffn1_w_out)

    half = RET_QK_DIM // 2
    inv = ROPE_BASE ** (-jnp.arange(half, dtype=F32) / half)
    ang = jnp.arange(seq, dtype=jnp.int32).astype(F32)[:, None] * inv[None, :]
    cos, sin = jnp.cos(ang), jnp.sin(ang)
    log_gamma = jnp.log1p(-jnp.exp2(-5.0 - jnp.arange(RET_HEADS, dtype=F32)))

    qkvg = _proj(h, l1_norm_mix, l1_ret_w_in, cos, sin, seq=seq)
    y = _retention(qkvg, l1_ret_gn, log_gamma, bsz=bsz, seq=seq)
    h = _out_proj(y, ret_w_out, h, tm=512, tn=512, name="ret_out_proj")

    h, = _ffn(h, l1_norm_ffn2, l1_ffn2_w_in, l1_ffn2_w_out, final_norm)
    return h.reshape(bsz, seq, d)
```

```python
import functools

import jax
import jax.numpy as jnp
from jax import lax
from jax.experimental import pallas as pl
from jax.experimental.pallas import tpu as pltpu

D_MODEL = 4096
D_FF = 11008
CHUNK = 64
CONV_WIDTH = 3
RET_HEADS = 16
RET_QK_DIM = D_MODEL // RET_HEADS
RET_V_DIM = 2 * D_MODEL // RET_HEADS
ROPE_BASE = 10000.0
EPS = 1e-6
GN_EPS = 1e-5

F32 = jnp.float32
BF16 = jnp.bfloat16

V7X_VMEM_LIMIT_BYTES = 60000 * 1024
LANES = 128


def _params(vmem_bytes):
    assert vmem_bytes <= V7X_VMEM_LIMIT_BYTES, vmem_bytes
    return pltpu.CompilerParams(
        dimension_semantics=("arbitrary", "arbitrary"),
        vmem_limit_bytes=V7X_VMEM_LIMIT_BYTES,
    )


NORM_ROWS = 16
NORM_UNROLL = 4
ROW_CHUNK = 128


def _rms(x, g):
    ms = jnp.mean(x * x, axis=-1, keepdims=True)
    return (x * lax.rsqrt(ms + EPS)) * g


def _rms_rows(src_ref, g_ref, dst_ref):
    g = g_ref[...]
    rows_per_step = NORM_ROWS * NORM_UNROLL
    assert src_ref.shape[0] % rows_per_step == 0

    def step(r, carry):
        base = pl.multiple_of(r * rows_per_step, rows_per_step)
        slabs = [pl.ds(base + u * NORM_ROWS, NORM_ROWS) for u in range(NORM_UNROLL)]
        xs = [src_ref[rows, :] for rows in slabs]
        for rows, x in zip(slabs, xs):
            dst_ref[rows, :] = _rms(x, g).astype(dst_ref.dtype)
        return carry

    lax.fori_loop(0, src_ref.shape[0] // rows_per_step, step, 0)


def _chunk_rows(c):
    return pl.ds(c * ROW_CHUNK, ROW_CHUNK)


def _tile_row0(tm):
    return pl.multiple_of(pl.program_id(0) * tm, tm)


def _stream_rms(h_hbm, g_ref, xn_ref, stage_ref, sem):
    tm = xn_ref.shape[0]
    chunk = stage_ref.shape[1]
    n_c = tm // chunk
    row0 = _tile_row0(tm)

    def copy(c):
        src = h_hbm.at[pl.ds(row0 + c * chunk, chunk), :]
        return pltpu.make_async_copy(src, stage_ref.at[c % 2], sem.at[c % 2])

    copy(0).start()
    for c in range(n_c):
        if c + 1 < n_c:
            copy(c + 1).start()
        copy(c).wait()
        _rms_rows(stage_ref.at[c % 2], g_ref, xn_ref.at[pl.ds(c * chunk, chunk), :])


def _stream_scratch(tm, d, chunk=ROW_CHUNK):
    assert tm % chunk == 0 and chunk % (NORM_ROWS * NORM_UNROLL) == 0
    return [pltpu.VMEM((tm, d), BF16), pltpu.VMEM((2, chunk, d), F32),
            pltpu.SemaphoreType.DMA((2,))]


def _silu(x):
    return x * jax.nn.sigmoid(x)


SIDE_BLOCKS = 256
BF16_SUBLANES = 16


def _side_cast_specs(weights, n_steps, step_of):
    n_blk = 1 << (min(n_steps, SIDE_BLOCKS).bit_length() - 1)
    specs, vmem = [], 0
    for w in weights:
        rows = w.shape[0] // n_blk
        assert w.shape[0] % n_blk == 0 and rows % BF16_SUBLANES == 0
        vmem += 2 * rows * w.shape[1] * (4 + 2)
        specs.append(pl.BlockSpec(
            (rows, w.shape[1]), lambda *idx: (jnp.minimum(step_of(*idx), n_blk - 1), 0)))
    return specs, vmem


def _side_cast(side_in, side_out):
    for src, dst in zip(side_in, side_out):
        dst[...] = src[...].astype(BF16)


OUT_COL_CHUNKS = 4


def _ffn_body(h_hbm, g_ref, wg_ref, wu_ref, wo_ref, fg_ref, *rest, n_j, final, n_side):
    side_in = rest[:n_side]
    out_hbm = rest[n_side]
    side_out = rest[n_side + 1:2 * n_side + 1]
    acc_ref, xn_ref, sem = rest[2 * n_side + 1:]
    j = pl.program_id(1)
    tm, d = acc_ref.shape
    n_c = tm // ROW_CHUNK
    row0 = _tile_row0(tm)

    def row_copy(c, *, load):
        hbm_rows = pl.ds(row0 + c * ROW_CHUNK, ROW_CHUNK)
        vm = acc_ref.at[_chunk_rows(c), :]
        if load:
            return pltpu.make_async_copy(h_hbm.at[hbm_rows, :], vm, sem.at[c])
        return pltpu.make_async_copy(vm, out_hbm.at[hbm_rows, :], sem.at[c])

    def col_copy(c):
        cols = pl.ds(c * (d // OUT_COL_CHUNKS), d // OUT_COL_CHUNKS)
        return pltpu.make_async_copy(acc_ref.at[:, cols], out_hbm.at[pl.ds(row0, tm), cols],
                                     sem.at[c])

    @pl.when(j == 0)
    def _():
        for c in range(n_c):
            row_copy(c, load=True).start()
        for c in range(n_c):
            row_copy(c, load=True).wait()
            _rms_rows(acc_ref.at[_chunk_rows(c), :], g_ref, xn_ref.at[_chunk_rows(c), :])

    def step(last):
        _side_cast(side_in, side_out)
        xn = xn_ref[...]
        gate = jnp.dot(xn, wg_ref[...].astype(BF16), preferred_element_type=F32)
        up = jnp.dot(xn, wu_ref[...].astype(BF16), preferred_element_type=F32)
        act = (0.5 * _silu(gate) * up).astype(BF16)
        wo = wo_ref[...].astype(BF16)
        if not last:
            acc_ref[...] += jnp.dot(act, wo, preferred_element_type=F32)
        elif final:
            acc_ref[...] += jnp.dot(act, wo, preferred_element_type=F32)
            for c in range(n_c):
                rows = acc_ref.at[_chunk_rows(c), :]
                _rms_rows(rows, fg_ref, rows)
                row_copy(c, load=False).start()
            for c in range(n_c):
                row_copy(c, load=False).wait()
        else:
            dc = d // OUT_COL_CHUNKS
            for c in range(OUT_COL_CHUNKS):
                cols = slice(c * dc, (c + 1) * dc)
                acc_ref[:, cols] += jnp.dot(act, wo[:, cols], preferred_element_type=F32)
                col_copy(c).start()
            for c in range(OUT_COL_CHUNKS):
                col_copy(c).wait()

    @pl.when(j < n_j - 1)
    def _():
        step(False)

    @pl.when(j == n_j - 1)
    def _():
        step(True)


def _ffn(h, g, w_in, w_out, final_g=None, side_weights=(), *, tm=1024, tf=256):
    m, d = h.shape
    d_ff = w_out.shape[0]
    n_j = d_ff // tf
    assert d_ff % tf == 0 and m % tm == 0 and tm % ROW_CHUNK == 0
    assert d % (OUT_COL_CHUNKS * LANES) == 0 and OUT_COL_CHUNKS <= tm // ROW_CHUNK
    final = final_g is not None
    fg = final_g if final else g
    side_specs, side_vmem = _side_cast_specs(side_weights, (m // tm) * n_j,
                                             lambda i, j: i * n_j + j)
    vmem = tm * d * 4 + tm * d * 2 + 2 * 3 * d * tf * 4 + side_vmem

    return pl.pallas_call(
        functools.partial(_ffn_body, n_j=n_j, final=final, n_side=len(side_weights)),
        grid=(m // tm, n_j),
        in_specs=[
            pl.BlockSpec(memory_space=pl.ANY),
            pl.BlockSpec((1, d), lambda i, j: (0, 0)),
            pl.BlockSpec((d, tf), lambda i, j: (0, j)),
            pl.BlockSpec((d, tf), lambda i, j: (0, j + n_j)),
            pl.BlockSpec((tf, d), lambda i, j: (j, 0)),
            pl.BlockSpec((1, d), lambda i, j: (0, 0)),
        ] + side_specs,
        out_specs=[pl.BlockSpec(memory_space=pl.ANY)] + side_specs,
        out_shape=[jax.ShapeDtypeStruct((m, d), F32)]
        + [jax.ShapeDtypeStruct(w.shape, BF16) for w in side_weights],
        scratch_shapes=[pltpu.VMEM((tm, d), F32), pltpu.VMEM((tm, d), BF16),
                        pltpu.SemaphoreType.DMA((tm // ROW_CHUNK,))],
        compiler_params=_params(vmem),
        name="ffn",
    )(h, g.reshape(1, d), w_in, w_in, w_out, fg.reshape(1, d), *side_weights)


def _conv_in_body(h_hbm, g_ref, wb_ref, wc_ref, wh_ref, cw_ref, *rest, tiles_per_seq, n_side):
    side_in = rest[:n_side]
    y_ref = rest[n_side]
    side_out = rest[n_side + 1:2 * n_side + 1]
    xn_ref, stage_ref, sem, carry_ref = rest[2 * n_side + 1:]
    i = pl.program_id(0)
    j = pl.program_id(1)
    _side_cast(side_in, side_out)

    @pl.when(j == 0)
    def _():
        _stream_rms(h_hbm, g_ref, xn_ref, stage_ref, sem)

    xn = xn_ref[...]
    b = jnp.dot(xn, wb_ref[...], preferred_element_type=F32)
    c = jnp.dot(xn, wc_ref[...], preferred_element_type=F32)
    hh = jnp.dot(xn, wh_ref[...], preferred_element_type=F32)
    u = c * hh
    tm = u.shape[0]

    @pl.when(i % tiles_per_seq == 0)
    def _():
        carry_ref[j] = jnp.zeros(carry_ref.shape[1:], F32)

    prev = carry_ref[j]
    carry_ref[j] = u[tm - 8:, :]
    row = lax.broadcasted_iota(jnp.int32, u.shape, 0)
    u1 = jnp.where(row < 1, prev[7:8, :], pltpu.roll(u, 1, 0))
    u2 = jnp.where(row < 2, jnp.where(row < 1, prev[6:7, :], prev[7:8, :]), pltpu.roll(u, 2, 0))
    cw = cw_ref[...]
    conv = cw[0:1, :] * u2 + cw[1:2, :] * u1 + cw[2:3, :] * u
    y_ref[...] = (b * conv).astype(BF16)


def _conv_in(h, g, w_in, conv_w, side_weights=(), *, seq, tm=1024, tc=512):
    m, d = h.shape
    n_j = d // tc
    assert m % tm == 0 and seq % tm == 0 and d % tc == 0
    side_specs, side_vmem = _side_cast_specs(side_weights, (m // tm) * n_j,
                                             lambda i, j: i * n_j + j)
    vmem = (2 * ROW_CHUNK * d * 4 + tm * d * 2 + 2 * 3 * d * tc * 2 + 2 * tm * tc * 2
            + n_j * 8 * tc * 4 + 6 * tm * tc * 4 + side_vmem)
    return pl.pallas_call(
        functools.partial(_conv_in_body, tiles_per_seq=seq // tm, n_side=len(side_weights)),
        grid=(m // tm, n_j),
        in_specs=[
            pl.BlockSpec(memory_space=pl.ANY),
            pl.BlockSpec((1, d), lambda i, j: (0, 0)),
            pl.BlockSpec((d, tc), lambda i, j: (0, j)),
            pl.BlockSpec((d, tc), lambda i, j: (0, j + n_j)),
            pl.BlockSpec((d, tc), lambda i, j: (0, j + 2 * n_j)),
            pl.BlockSpec((CONV_WIDTH, tc), lambda i, j: (0, j)),
        ] + side_specs,
        out_specs=[pl.BlockSpec((tm, tc), lambda i, j: (i, j))] + side_specs,
        out_shape=[jax.ShapeDtypeStruct((m, d), BF16)]
        + [jax.ShapeDtypeStruct(w.shape, BF16) for w in side_weights],
        scratch_shapes=_stream_scratch(tm, d) + [pltpu.VMEM((n_j, 8, tc), F32)],
        compiler_params=_params(vmem),
        name="conv_in",
    )(h, g.reshape(1, d), w_in, w_in, w_in, conv_w, *side_weights)


PROJ_ROW_CHUNK = 64


def _proj_body(h_hbm, g_ref, w_ref, cos_ref, sin_ref, out_ref, xn_ref, stage_ref, sem,
               *, n_q_tiles):
    j = pl.program_id(1)

    @pl.when(j == 0)
    def _():
        _stream_rms(h_hbm, g_ref, xn_ref, stage_ref, sem)

    def project():
        return jnp.dot(xn_ref[...], w_ref[...].astype(BF16), preferred_element_type=F32)

    @pl.when(j < 2 * n_q_tiles)
    def _():
        t = project()
        cos = cos_ref[...]
        sin = sin_ref[...]
        scale = jnp.where(j < n_q_tiles, RET_QK_DIM ** -0.5, 1.0)
        half = RET_QK_DIM // 2
        for hd in range(t.shape[1] // RET_QK_DIM):
            lo = hd * RET_QK_DIM
            t1 = t[:, lo:lo + half]
            t2 = t[:, lo + half:lo + 2 * half]
            out_ref[:, lo:lo + half] = ((t1 * cos - t2 * sin) * scale).astype(BF16)
            out_ref[:, lo + half:lo + 2 * half] = ((t1 * sin + t2 * cos) * scale).astype(BF16)

    @pl.when(j >= 2 * n_q_tiles)
    def _():
        out_ref[...] = project().astype(BF16)


def _proj(h, g, w, cos, sin, *, seq, tm=1024, tn=1024):
    m, d = h.shape
    n = w.shape[1]
    assert m % tm == 0 and n % tn == 0 and d % tn == 0 and seq % tm == 0
    half = RET_QK_DIM // 2
    assert half == LANES and tn % RET_QK_DIM == 0
    tiles_per_seq = seq // tm
    vmem = (2 * PROJ_ROW_CHUNK * d * 4 + tm * d * 2 + 2 * d * tn * 4 + 2 * tm * tn * 2
            + 4 * tm * half * 4)
    table_spec = pl.BlockSpec((tm, half), lambda i, j: (i % tiles_per_seq, 0))
    return pl.pallas_call(
        functools.partial(_proj_body, n_q_tiles=d // tn),
        grid=(m // tm, n // tn),
        in_specs=[
            pl.BlockSpec(memory_space=pl.ANY),
            pl.BlockSpec((1, d), lambda i, j: (0, 0)),
            pl.BlockSpec((d, tn), lambda i, j: (0, j)),
            table_spec,
            table_spec,
        ],
        out_specs=pl.BlockSpec((tm, tn), lambda i, j: (i, j)),
        out_shape=jax.ShapeDtypeStruct((m, n), BF16),
        scratch_shapes=_stream_scratch(tm, d, PROJ_ROW_CHUNK),
        compiler_params=_params(vmem),
        name="ret_proj",
    )(h, g.reshape(1, d), w, cos, sin)


def _ret_body(lg_ref, q_ref, k_ref, v_ref, gate_ref, gn_ref, *rest, t_blk, heads, n_side):
    side_in = rest[:n_side]
    out_ref = rest[n_side]
    side_out = rest[n_side + 1:2 * n_side + 1]
    state_ref, dmat_ref, qdec_ref, kdec_ref = rest[2 * n_side + 1:]
    hg = pl.program_id(1)
    t = pl.program_id(2)
    _side_cast(side_in, side_out)

    @pl.when(t == 0)
    def _():
        state_ref[...] = jnp.zeros_like(state_ref)
        row = lax.broadcasted_iota(jnp.int32, (t_blk, t_blk), 0)
        col = lax.broadcasted_iota(jnp.int32, (t_blk, t_blk), 1)
        shift = CHUNK.bit_length() - 1
        rc = lax.shift_right_logical(row, shift)
        cc = lax.shift_right_logical(col, shift)
        diff = row - col
        visible = cc <= rc
        dist = jnp.where(visible, jnp.where(rc == cc, jnp.abs(diff), diff), 0).astype(F32)
        idx = lax.broadcasted_iota(jnp.int32, (t_blk, RET_QK_DIM), 0).astype(F32)
        for hh in range(heads):
            lg = lg_ref[hg * heads + hh]
            dmat_ref[hh] = jnp.where(visible, jnp.exp(lg * dist), 0.0)
            qdec_ref[hh] = jnp.exp(lg * (idx + 1.0))
            kdec_ref[hh] = jnp.exp(lg * (t_blk - 1.0 - idx))

    for hh in range(heads):
        qk_cols = slice(hh * RET_QK_DIM, (hh + 1) * RET_QK_DIM)
        v_cols = slice(hh * RET_V_DIM, (hh + 1) * RET_V_DIM)
        q = q_ref[:, qk_cols]
        k = k_ref[:, qk_cols]
        v = v_ref[:, v_cols]
        s = lax.dot_general(q, k, (((1,), (1,)), ((), ())), preferred_element_type=F32)
        p = (s * dmat_ref[hh]).astype(BF16)
        o = jnp.dot(p, v, preferred_element_type=F32)

        qx = (q.astype(F32) * qdec_ref[hh]).astype(BF16)
        kxt = (k.astype(F32) * kdec_ref[hh]).T.astype(BF16)
        st = state_ref[hh]
        o = o + jnp.dot(qx, st.astype(BF16), preferred_element_type=F32)
        blk_dec = qdec_ref[hh, t_blk - 1:t_blk, 0:1]
        state_ref[hh] = blk_dec * st + jnp.dot(kxt, v, preferred_element_type=F32)

        mu = jnp.mean(o, axis=-1, keepdims=True)
        dlt = o - mu
        var = jnp.mean(dlt * dlt, axis=-1, keepdims=True)
        y = (dlt * lax.rsqrt(var + GN_EPS)) * gn_ref[:, v_cols]
        out_ref[:, v_cols] = (_silu(gate_ref[:, v_cols].astype(F32)) * y).astype(BF16)


def _retention(qkvg, gn_g, log_gamma, side_weights=(), *, bsz, seq, t_blk=256, heads=8):
    m = qkvg.shape[0]
    nt = seq // t_blk
    n_hg = RET_HEADS // heads
    assert seq % t_blk == 0 and t_blk % CHUNK == 0 and RET_HEADS % heads == 0
    qk_w = heads * RET_QK_DIM
    v_w = heads * RET_V_DIM
    side_specs, _ = _side_cast_specs(side_weights, bsz * n_hg * nt,
                                     lambda b, h, t: (b * n_hg + h) * nt + t)
    return pl.pallas_call(
        functools.partial(_ret_body, t_blk=t_blk, heads=heads, n_side=len(side_weights)),
        grid=(bsz, n_hg, nt),
        in_specs=[
            pl.BlockSpec(memory_space=pltpu.SMEM),
            pl.BlockSpec((t_blk, qk_w), lambda b, h, t: (b * nt + t, h)),
            pl.BlockSpec((t_blk, qk_w), lambda b, h, t: (b * nt + t, h + n_hg)),
            pl.BlockSpec((t_blk, v_w), lambda b, h, t: (b * nt + t, h + n_hg)),
            pl.BlockSpec((t_blk, v_w), lambda b, h, t: (b * nt + t, h + 2 * n_hg)),
            pl.BlockSpec((1, v_w), lambda b, h, t: (0, h)),
        ] + side_specs,
        out_specs=[pl.BlockSpec((t_blk, v_w), lambda b, h, t: (b * nt + t, h))] + side_specs,
        out_shape=[jax.ShapeDtypeStruct((m, RET_HEADS * RET_V_DIM), BF16)]
        + [jax.ShapeDtypeStruct(w.shape, BF16) for w in side_weights],
        scratch_shapes=[pltpu.VMEM((heads, RET_QK_DIM, RET_V_DIM), F32),
                        pltpu.VMEM((heads, t_blk, t_blk), F32),
                        pltpu.VMEM((heads, t_blk, RET_QK_DIM), F32),
                        pltpu.VMEM((heads, t_blk, RET_QK_DIM), F32)],
        compiler_params=pltpu.CompilerParams(
            dimension_semantics=("arbitrary", "arbitrary", "arbitrary")),
        name="retention",
    )(log_gamma, qkvg, qkvg, qkvg, qkvg, gn_g.reshape(1, -1), *side_weights)


def _out_proj_body(y_ref, w_ref, h_ref, out_ref):
    out_ref[...] = h_ref[...] + jnp.dot(y_ref[...], w_ref[...], preferred_element_type=F32)


def _out_proj(y, w, h, *, tm, tn, name):
    m, kdim = y.shape
    d = w.shape[1]
    assert m % tm == 0 and d % tn == 0
    assert y.dtype == BF16 and w.dtype == BF16
    vmem = 2 * tm * kdim * 2 + 2 * kdim * tn * 2 + 4 * tm * tn * 4
    return pl.pallas_call(
        _out_proj_body,
        grid=(m // tm, d // tn),
        in_specs=[
            pl.BlockSpec((tm, kdim), lambda i, j: (i, 0)),
            pl.BlockSpec((kdim, tn), lambda i, j: (0, j)),
            pl.BlockSpec((tm, tn), lambda i, j: (i, j)),
        ],
        out_specs=pl.BlockSpec((tm, tn), lambda i, j: (i, j)),
        out_shape=jax.ShapeDtypeStruct((m, d), F32),
        compiler_params=_params(vmem),
        name=name,
    )(y, w, h)


def kernel(x, l0_norm_ffn1, l0_ffn1_w_in, l0_ffn1_w_out, l0_norm_mix, l0_conv_w_in, l0_conv_w, l0_conv_w_out, l0_norm_ffn2, l0_ffn2_w_in, l0_ffn2_w_out, l1_norm_ffn1, l1_ffn1_w_in, l1_ffn1_w_out, l1_norm_mix, l1_ret_w_in, l1_ret_gn, l1_ret_w_out, l1_norm_ffn2, l1_ffn2_w_in, l1_ffn2_w_out, final_norm):
    bsz, seq, d = x.shape
    h = x.reshape(bsz * seq, d)

    h, conv_w_in = _ffn(h, l0_norm_ffn1, l0_ffn1_w_in, l0_ffn1_w_out,
                        side_weights=(l0_conv_w_in,))
    y, conv_w_out = _conv_in(h, l0_norm_mix, conv_w_in, l0_conv_w,
                             side_weights=(l0_conv_w_out,), seq=seq)
    h = _out_proj(y, conv_w_out, h, tm=1024, tn=512, name="conv_out_proj")
    h, = _ffn(h, l0_norm_ffn2, l0_ffn2_w_in, l0_ffn2_w_out)

    h, = _ffn(h, l1_norm_ffn1, l1_ffn1_w_in, l1_ffn1_w_out)

    half = RET_QK_DIM // 2
    inv = ROPE_BASE ** (-jnp.arange(half, dtype=F32) / half)
    ang = jnp.arange(seq, dtype=jnp.int32).astype(F32)[:, None] * inv[None, :]
    cos, sin = jnp.cos(ang), jnp.sin(ang)
    log_gamma = jnp.log1p(-jnp.exp2(-5.0 - jnp.arange(RET_HEADS, dtype=F32)))

    qkvg = _proj(h, l1_norm_mix, l1_ret_w_in, cos, sin, seq=seq)
    y, ret_w_out = _retention(qkvg, l1_ret_gn, log_gamma, side_weights=(l1_ret_w_out,),
                              bsz=bsz, seq=seq)
    h = _out_proj(y, ret_w_out, h, tm=512, tn=512, name="ret_out_proj")

    h, = _ffn(h, l1_norm_ffn2, l1_ffn2_w_in, l1_ffn2_w_out, final_norm)
    return h.reshape(bsz, seq, d)
```

```python
import functools

import jax
import jax.numpy as jnp
from jax import lax
from jax.experimental import pallas as pl
from jax.experimental.pallas import tpu as pltpu

D_MODEL = 4096
D_FF = 11008
CHUNK = 64
CONV_WIDTH = 3
RET_HEADS = 16
RET_QK_DIM = D_MODEL // RET_HEADS
RET_V_DIM = 2 * D_MODEL // RET_HEADS
ROPE_BASE = 10000.0
EPS = 1e-6
GN_EPS = 1e-5

F32 = jnp.float32
BF16 = jnp.bfloat16

V7X_VMEM_LIMIT_BYTES = 60000 * 1024
LANES = 128


def _params(vmem_bytes):
    assert vmem_bytes <= V7X_VMEM_LIMIT_BYTES, vmem_bytes
    return pltpu.CompilerParams(
        dimension_semantics=("arbitrary", "arbitrary"),
        vmem_limit_bytes=V7X_VMEM_LIMIT_BYTES,
    )


NORM_ROWS = 16
NORM_UNROLL = 4
ROW_CHUNK = 128


def _rms(x, g):
    ms = jnp.mean(x * x, axis=-1, keepdims=True)
    return (x * lax.rsqrt(ms + EPS)) * g


def _rms_rows(src_ref, g_ref, dst_ref):
    g = g_ref[...]
    rows_per_step = NORM_ROWS * NORM_UNROLL
    assert src_ref.shape[0] % rows_per_step == 0

    def step(r, carry):
        base = pl.multiple_of(r * rows_per_step, rows_per_step)
        slabs = [pl.ds(base + u * NORM_ROWS, NORM_ROWS) for u in range(NORM_UNROLL)]
        xs = [src_ref[rows, :] for rows in slabs]
        for rows, x in zip(slabs, xs):
            dst_ref[rows, :] = _rms(x, g).astype(dst_ref.dtype)
        return carry

    lax.fori_loop(0, src_ref.shape[0] // rows_per_step, step, 0)


def _chunk_rows(c):
    return pl.ds(c * ROW_CHUNK, ROW_CHUNK)


def _tile_row0(tm):
    return pl.multiple_of(pl.program_id(0) * tm, tm)


def _stream_rms(h_hbm, g_ref, xn_ref, stage_ref, sem):
    tm = xn_ref.shape[0]
    chunk = stage_ref.shape[1]
    n_c = tm // chunk
    row0 = _tile_row0(tm)

    def copy(c):
        src = h_hbm.at[pl.ds(row0 + c * chunk, chunk), :]
        return pltpu.make_async_copy(src, stage_ref.at[c % 2], sem.at[c % 2])

    copy(0).start()
    for c in range(n_c):
        if c + 1 < n_c:
            copy(c + 1).start()
        copy(c).wait()
        _rms_rows(stage_ref.at[c % 2], g_ref, xn_ref.at[pl.ds(c * chunk, chunk), :])


def _stream_scratch(tm, d, chunk=ROW_CHUNK):
    assert tm % chunk == 0 and chunk % (NORM_ROWS * NORM_UNROLL) == 0
    return [pltpu.VMEM((tm, d), BF16), pltpu.VMEM((2, chunk, d), F32),
            pltpu.SemaphoreType.DMA((2,))]


def _silu(x):
    return x * jax.nn.sigmoid(x)


SIDE_BLOCKS = 256
BF16_SUBLANES = 16


def _side_cast_specs(weights, n_steps, step_of):
    n_blk = 1 << (min(n_steps, SIDE_BLOCKS).bit_length() - 1)
    specs, vmem = [], 0
    for w in weights:
        rows = w.shape[0] // n_blk
        assert w.shape[0] % n_blk == 0 and rows % BF16_SUBLANES == 0
        vmem += 2 * rows * w.shape[1] * (4 + 2)
        specs.append(pl.BlockSpec(
            (rows, w.shape[1]), lambda *idx: (jnp.minimum(step_of(*idx), n_blk - 1), 0)))
    return specs, vmem


def _side_cast(side_in, side_out):
    for src, dst in zip(side_in, side_out):
        dst[...] = src[...].astype(BF16)


OUT_COL_CHUNKS = 8


def _ffn_body(h_hbm, g_ref, wg_ref, wu_ref, wo_ref, fg_ref, *rest, n_j, final, n_side):
    side_in = rest[:n_side]
    out_hbm = rest[n_side]
    side_out = rest[n_side + 1:2 * n_side + 1]
    acc_ref, xn_ref, sem = rest[2 * n_side + 1:]
    j = pl.program_id(1)
    tm, d = acc_ref.shape
    n_c = tm // ROW_CHUNK
    row0 = _tile_row0(tm)

    def row_copy(c, *, load):
        hbm_rows = pl.ds(row0 + c * ROW_CHUNK, ROW_CHUNK)
        vm = acc_ref.at[_chunk_rows(c), :]
        if load:
            return pltpu.make_async_copy(h_hbm.at[hbm_rows, :], vm, sem.at[c])
        return pltpu.make_async_copy(vm, out_hbm.at[hbm_rows, :], sem.at[c])

    def col_copy(c):
        cols = pl.ds(c * (d // OUT_COL_CHUNKS), d // OUT_COL_CHUNKS)
        return pltpu.make_async_copy(acc_ref.at[:, cols], out_hbm.at[pl.ds(row0, tm), cols],
                                     sem.at[c])

    @pl.when(j == 0)
    def _():
        for c in range(n_c):
            row_copy(c, load=True).start()
        for c in range(n_c):
            row_copy(c, load=True).wait()
            _rms_rows(acc_ref.at[_chunk_rows(c), :], g_ref, xn_ref.at[_chunk_rows(c), :])

    def step(last):
        _side_cast(side_in, side_out)
        xn = xn_ref[...]
        gate = jnp.dot(xn, wg_ref[...].astype(BF16), preferred_element_type=F32)
        up = jnp.dot(xn, wu_ref[...].astype(BF16), preferred_element_type=F32)
        act = (0.5 * _silu(gate) * up).astype(BF16)
        wo = wo_ref[...].astype(BF16)
        if not last:
            acc_ref[...] += jnp.dot(act, wo, preferred_element_type=F32)
        elif final:
            acc_ref[...] += jnp.dot(act, wo, preferred_element_type=F32)
            for c in range(n_c):
                rows = acc_ref.at[_chunk_rows(c), :]
                _rms_rows(rows, fg_ref, rows)
                row_copy(c, load=False).start()
            for c in range(n_c):
                row_copy(c, load=False).wait()
        else:
            dc = d // OUT_COL_CHUNKS
            for c in range(OUT_COL_CHUNKS):
                cols = slice(c * dc, (c + 1) * dc)
                acc_ref[:, cols] += jnp.dot(act, wo[:, cols], preferred_element_type=F32)
                col_copy(c).start()
            for c in range(OUT_COL_CHUNKS):
                col_copy(c).wait()

    @pl.when(j < n_j - 1)
    def _():
        step(False)

    @pl.when(j == n_j - 1)
    def _():
        step(True)


def _ffn(h, g, w_in, w_out, final_g=None, side_weights=(), *, tm=1024, tf=256):
    m, d = h.shape
    d_ff = w_out.shape[0]
    n_j = d_ff // tf
    assert d_ff % tf == 0 and m % tm == 0 and tm % ROW_CHUNK == 0
    assert d % (OUT_COL_CHUNKS * LANES) == 0 and OUT_COL_CHUNKS <= tm // ROW_CHUNK
    final = final_g is not None
    fg = final_g if final else g
    side_specs, side_vmem = _side_cast_specs(side_weights, (m // tm) * n_j,
                                             lambda i, j: i * n_j + j)
    vmem = tm * d * 4 + tm * d * 2 + 2 * 3 * d * tf * 4 + side_vmem

    return pl.pallas_call(
        functools.partial(_ffn_body, n_j=n_j, final=final, n_side=len(side_weights)),
        grid=(m // tm, n_j),
        in_specs=[
            pl.BlockSpec(memory_space=pl.ANY),
            pl.BlockSpec((1, d), lambda i, j: (0, 0)),
            pl.BlockSpec((d, tf), lambda i, j: (0, j)),
            pl.BlockSpec((d, tf), lambda i, j: (0, j + n_j)),
            pl.BlockSpec((tf, d), lambda i, j: (j, 0)),
            pl.BlockSpec((1, d), lambda i, j: (0, 0)),
        ] + side_specs,
        out_specs=[pl.BlockSpec(memory_space=pl.ANY)] + side_specs,
        out_shape=[jax.ShapeDtypeStruct((m, d), F32)]
        + [jax.ShapeDtypeStruct(w.shape, BF16) for w in side_weights],
        scratch_shapes=[pltpu.VMEM((tm, d), F32), pltpu.VMEM((tm, d), BF16),
                        pltpu.SemaphoreType.DMA((tm // ROW_CHUNK,))],
        compiler_params=_params(vmem),
        name="ffn",
    )(h, g.reshape(1, d), w_in, w_in, w_out, fg.reshape(1, d), *side_weights)


CONV_ROW_SPLITS = 4


def _conv_in_body(h_hbm, g_ref, wb_ref, wc_ref, wh_ref, cw_ref, *rest, tiles_per_seq, n_side):
    side_in = rest[:n_side]
    y_ref = rest[n_side]
    side_out = rest[n_side + 1:2 * n_side + 1]
    xn_ref, stage_ref, sem, carry_ref = rest[2 * n_side + 1:]
    i = pl.program_id(0)
    j = pl.program_id(1)
    _side_cast(side_in, side_out)

    @pl.when(j == 0)
    def _():
        _stream_rms(h_hbm, g_ref, xn_ref, stage_ref, sem)

    @pl.when(i % tiles_per_seq == 0)
    def _():
        carry_ref[j] = jnp.zeros(carry_ref.shape[1:], F32)

    cw = cw_ref[...]
    hm = xn_ref.shape[0] // CONV_ROW_SPLITS
    prev = carry_ref[j]
    for r in range(CONV_ROW_SPLITS):
        rows = slice(r * hm, (r + 1) * hm)
        xn = xn_ref[rows, :]
        b = jnp.dot(xn, wb_ref[...], preferred_element_type=F32)
        c = jnp.dot(xn, wc_ref[...], preferred_element_type=F32)
        hh = jnp.dot(xn, wh_ref[...], preferred_element_type=F32)
        u = c * hh
        row = lax.broadcasted_iota(jnp.int32, u.shape, 0)
        u1 = jnp.where(row < 1, prev[7:8, :], pltpu.roll(u, 1, 0))
        u2 = jnp.where(row < 2, jnp.where(row < 1, prev[6:7, :], prev[7:8, :]),
                       pltpu.roll(u, 2, 0))
        conv = cw[0:1, :] * u2 + cw[1:2, :] * u1 + cw[2:3, :] * u
        y_ref[rows, :] = (b * conv).astype(BF16)
        prev = u[hm - 8:, :]
    carry_ref[j] = prev


def _conv_in(h, g, w_in, conv_w, side_weights=(), *, seq, tm=1024, tc=512):
    m, d = h.shape
    n_j = d // tc
    assert m % tm == 0 and seq % tm == 0 and d % tc == 0
    side_specs, side_vmem = _side_cast_specs(side_weights, (m // tm) * n_j,
                                             lambda i, j: i * n_j + j)
    vmem = (2 * ROW_CHUNK * d * 4 + tm * d * 2 + 2 * 3 * d * tc * 2 + 2 * tm * tc * 2
            + n_j * 8 * tc * 4 + 6 * tm * tc * 4 + side_vmem)
    return pl.pallas_call(
        functools.partial(_conv_in_body, tiles_per_seq=seq // tm, n_side=len(side_weights)),
        grid=(m // tm, n_j),
        in_specs=[
            pl.BlockSpec(memory_space=pl.ANY),
            pl.BlockSpec((1, d), lambda i, j: (0, 0)),
            pl.BlockSpec((d, tc), lambda i, j: (0, j)),
            pl.BlockSpec((d, tc), lambda i, j: (0, j + n_j)),
            pl.BlockSpec((d, tc), lambda i, j: (0, j + 2 * n_j)),
            pl.BlockSpec((CONV_WIDTH, tc), lambda i, j: (0, j)),
        ] + side_specs,
        out_specs=[pl.BlockSpec((tm, tc), lambda i, j: (i, j))] + side_specs,
        out_shape=[jax.ShapeDtypeStruct((m, d), BF16)]
        + [jax.ShapeDtypeStruct(w.shape, BF16) for w in side_weights],
        scratch_shapes=_stream_scratch(tm, d) + [pltpu.VMEM((n_j, 8, tc), F32)],
        compiler_params=_params(vmem),
        name="conv_in",
    )(h, g.reshape(1, d), w_in, w_in, w_in, conv_w, *side_weights)


PROJ_ROW_CHUNK = 64


def _proj_body(h_hbm, g_ref, w_ref, cos_ref, sin_ref, out_ref, xn_ref, stage_ref, sem,
               *, n_q_tiles):
    j = pl.program_id(1)

    @pl.when(j == 0)
    def _():
        _stream_rms(h_hbm, g_ref, xn_ref, stage_ref, sem)

    def project():
        return jnp.dot(xn_ref[...], w_ref[...].astype(BF16), preferred_element_type=F32)

    @pl.when(j < 2 * n_q_tiles)
    def _():
        t = project()
        cos = cos_ref[...]
        sin = sin_ref[...]
        scale = jnp.where(j < n_q_tiles, RET_QK_DIM ** -0.5, 1.0)
        half = RET_QK_DIM // 2
        for hd in range(t.shape[1] // RET_QK_DIM):
            lo = hd * RET_QK_DIM
            t1 = t[:, lo:lo + half]
            t2 = t[:, lo + half:lo + 2 * half]
            out_ref[:, lo:lo + half] = ((t1 * cos - t2 * sin) * scale).astype(BF16)
            out_ref[:, lo + half:lo + 2 * half] = ((t1 * sin + t2 * cos) * scale).astype(BF16)

    @pl.when(j >= 2 * n_q_tiles)
    def _():
        out_ref[...] = project().astype(BF16)


def _proj(h, g, w, cos, sin, *, seq, tm=1024, tn=1024):
    m, d = h.shape
    n = w.shape[1]
    assert m % tm == 0 and n % tn == 0 and d % tn == 0 and seq % tm == 0
    half = RET_QK_DIM // 2
    assert half == LANES and tn % RET_QK_DIM == 0
    tiles_per_seq = seq // tm
    vmem = (2 * PROJ_ROW_CHUNK * d * 4 + tm * d * 2 + 2 * d * tn * 4 + 2 * tm * tn * 2
            + 4 * tm * half * 4)
    table_spec = pl.BlockSpec((tm, half), lambda i, j: (i % tiles_per_seq, 0))
    return pl.pallas_call(
        functools.partial(_proj_body, n_q_tiles=d // tn),
        grid=(m // tm, n // tn),
        in_specs=[
            pl.BlockSpec(memory_space=pl.ANY),
            pl.BlockSpec((1, d), lambda i, j: (0, 0)),
            pl.BlockSpec((d, tn), lambda i, j: (0, j)),
            table_spec,
            table_spec,
        ],
        out_specs=pl.BlockSpec((tm, tn), lambda i, j: (i, j)),
        out_shape=jax.ShapeDtypeStruct((m, n), BF16),
        scratch_shapes=_stream_scratch(tm, d, PROJ_ROW_CHUNK),
        compiler_params=_params(vmem),
        name="ret_proj",
    )(h, g.reshape(1, d), w, cos, sin)


def _ret_body(lg_ref, q_ref, k_ref, v_ref, gate_ref, gn_ref, *rest, t_blk, heads, n_side):
    side_in = rest[:n_side]
    out_ref = rest[n_side]
    side_out = rest[n_side + 1:2 * n_side + 1]
    state_ref, dmat_ref, qdec_ref, kdec_ref = rest[2 * n_side + 1:]
    hg = pl.program_id(1)
    t = pl.program_id(2)
    _side_cast(side_in, side_out)

    @pl.when(t == 0)
    def _():
        state_ref[...] = jnp.zeros_like(state_ref)
        row = lax.broadcasted_iota(jnp.int32, (t_blk, t_blk), 0)
        col = lax.broadcasted_iota(jnp.int32, (t_blk, t_blk), 1)
        shift = CHUNK.bit_length() - 1
        rc = lax.shift_right_logical(row, shift)
        cc = lax.shift_right_logical(col, shift)
        diff = row - col
        visible = cc <= rc
        dist = jnp.where(visible, jnp.where(rc == cc, jnp.abs(diff), diff), 0).astype(F32)
        idx = lax.broadcasted_iota(jnp.int32, (t_blk, RET_QK_DIM), 0).astype(F32)
        for hh in range(heads):
            lg = lg_ref[hg * heads + hh]
            dmat_ref[hh] = jnp.where(visible, jnp.exp(lg * dist), 0.0)
            qdec_ref[hh] = jnp.exp(lg * (idx + 1.0))
            kdec_ref[hh] = jnp.exp(lg * (t_blk - 1.0 - idx))

    for hh in range(heads):
        qk_cols = slice(hh * RET_QK_DIM, (hh + 1) * RET_QK_DIM)
        v_cols = slice(hh * RET_V_DIM, (hh + 1) * RET_V_DIM)
        q = q_ref[:, qk_cols]
        k = k_ref[:, qk_cols]
        v = v_ref[:, v_cols]
        s = lax.dot_general(q, k, (((1,), (1,)), ((), ())), preferred_element_type=F32)
        p = (s * dmat_ref[hh]).astype(BF16)
        o = jnp.dot(p, v, preferred_element_type=F32)

        qx = (q.astype(F32) * qdec_ref[hh]).astype(BF16)
        kxt = (k.astype(F32) * kdec_ref[hh]).T.astype(BF16)
        st = state_ref[hh]
        o = o + jnp.dot(qx, st.astype(BF16), preferred_element_type=F32)
        blk_dec = qdec_ref[hh, t_blk - 1:t_blk, 0:1]
        state_ref[hh] = blk_dec * st + jnp.dot(kxt, v, preferred_element_type=F32)

        mu = jnp.mean(o, axis=-1, keepdims=True)
        dlt = o - mu
        var = jnp.mean(dlt * dlt, axis=-1, keepdims=True)
        y = (dlt * lax.rsqrt(var + GN_EPS)) * gn_ref[:, v_cols]
        out_ref[:, v_cols] = (_silu(gate_ref[:, v_cols].astype(F32)) * y).astype(BF16)


def _retention(qkvg, gn_g, log_gamma, side_weights=(), *, bsz, seq, t_blk=256, heads=8):
    m = qkvg.shape[0]
    nt = seq // t_blk
    n_hg = RET_HEADS // heads
    assert seq % t_blk == 0 and t_blk % CHUNK == 0 and RET_HEADS % heads == 0
    qk_w = heads * RET_QK_DIM
    v_w = heads * RET_V_DIM
    side_specs, _ = _side_cast_specs(side_weights, bsz * n_hg * nt,
                                     lambda b, h, t: (b * n_hg + h) * nt + t)
    return pl.pallas_call(
        functools.partial(_ret_body, t_blk=t_blk, heads=heads, n_side=len(side_weights)),
        grid=(bsz, n_hg, nt),
        in_specs=[
            pl.BlockSpec(memory_space=pltpu.SMEM),
            pl.BlockSpec((t_blk, qk_w), lambda b, h, t: (b * nt + t, h)),
            pl.BlockSpec((t_blk, qk_w), lambda b, h, t: (b * nt + t, h + n_hg)),
            pl.BlockSpec((t_blk, v_w), lambda b, h, t: (b * nt + t, h + n_hg)),
            pl.BlockSpec((t_blk, v_w), lambda b, h, t: (b * nt + t, h + 2 * n_hg)),
            pl.BlockSpec((1, v_w), lambda b, h, t: (0, h)),
        ] + side_specs,
        out_specs=[pl.BlockSpec((t_blk, v_w), lambda b, h, t: (b * nt + t, h))] + side_specs,
        out_shape=[jax.ShapeDtypeStruct((m, RET_HEADS * RET_V_DIM), BF16)]
        + [jax.ShapeDtypeStruct(w.shape, BF16) for w in side_weights],
        scratch_shapes=[pltpu.VMEM((heads, RET_QK_DIM, RET_V_DIM), F32),
                        pltpu.VMEM((heads, t_blk, t_blk), F32),
                        pltpu.VMEM((heads, t_blk, RET_QK_DIM), F32),
                        pltpu.VMEM((heads, t_blk, RET_QK_DIM), F32)],
        compiler_params=pltpu.CompilerParams(
            dimension_semantics=("arbitrary", "arbitrary", "arbitrary")),
        name="retention",
    )(log_gamma, qkvg, qkvg, qkvg, qkvg, gn_g.reshape(1, -1), *side_weights)


def _out_proj_body(y_ref, w_ref, h_ref, out_ref):
    out_ref[...] = h_ref[...] + jnp.dot(y_ref[...], w_ref[...], preferred_element_type=F32)


def _out_proj(y, w, h, *, tm, tn, name):
    m, kdim = y.shape
    d = w.shape[1]
    assert m % tm == 0 and d % tn == 0
    assert y.dtype == BF16 and w.dtype == BF16
    vmem = 2 * tm * kdim * 2 + 2 * kdim * tn * 2 + 4 * tm * tn * 4
    return pl.pallas_call(
        _out_proj_body,
        grid=(m // tm, d // tn),
        in_specs=[
            pl.BlockSpec((tm, kdim), lambda i, j: (i, 0)),
            pl.BlockSpec((kdim, tn), lambda i, j: (0, j)),
            pl.BlockSpec((tm, tn), lambda i, j: (i, j)),
        ],
        out_specs=pl.BlockSpec((tm, tn), lambda i, j: (i, j)),
        out_shape=jax.ShapeDtypeStruct((m, d), F32),
        compiler_params=_params(vmem),
        name=name,
    )(y, w, h)


def kernel(x, l0_norm_ffn1, l0_ffn1_w_in, l0_ffn1_w_out, l0_norm_mix, l0_conv_w_in, l0_conv_w, l0_conv_w_out, l0_norm_ffn2, l0_ffn2_w_in, l0_ffn2_w_out, l1_norm_ffn1, l1_ffn1_w_in, l1_ffn1_w_out, l1_norm_mix, l1_ret_w_in, l1_ret_gn, l1_ret_w_out, l1_norm_ffn2, l1_ffn2_w_in, l1_ffn2_w_out, final_norm):
    bsz, seq, d = x.shape
    h = x.reshape(bsz * seq, d)

    h, conv_w_in = _ffn(h, l0_norm_ffn1, l0_ffn1_w_in, l0_ffn1_w_out,
                        side_weights=(l0_conv_w_in,))
    y, conv_w_out = _conv_in(h, l0_norm_mix, conv_w_in, l0_conv_w,
                             side_weights=(l0_conv_w_out,), seq=seq)
    h = _out_proj(y, conv_w_out, h, tm=1024, tn=512, name="conv_out_proj")
    h, = _ffn(h, l0_norm_ffn2, l0_ffn2_w_in, l0_ffn2_w_out)

    h, = _ffn(h, l1_norm_ffn1, l1_ffn1_w_in, l1_ffn1_w_out)

    half = RET_QK_DIM // 2
    inv = ROPE_BASE ** (-jnp.arange(half, dtype=F32) / half)
    ang = jnp.arange(seq, dtype=jnp.int32).astype(F32)[:, None] * inv[None, :]
    cos, sin = jnp.cos(ang), jnp.sin(ang)
    log_gamma = jnp.log1p(-jnp.exp2(-5.0 - jnp.arange(RET_HEADS, dtype=F32)))

    qkvg = _proj(h, l1_norm_mix, l1_ret_w_in, cos, sin, seq=seq)
    y, ret_w_out = _retention(qkvg, l1_ret_gn, log_gamma, side_weights=(l1_ret_w_out,),
                              bsz=bsz, seq=seq)
    h = _out_proj(y, ret_w_out, h, tm=512, tn=512, name="ret_out_proj")

    h, = _ffn(h, l1_norm_ffn2, l1_ffn2_w_in, l1_ffn2_w_out, final_norm)
    return h.reshape(bsz, seq, d)
```

```python
import functools

import jax
import jax.numpy as jnp
from jax import lax
from jax.experimental import pallas as pl
from jax.experimental.pallas import tpu as pltpu

D_MODEL = 4096
CHUNK = 64
CONV_WIDTH = 3
RET_HEADS = 16
RET_QK_DIM = D_MODEL // RET_HEADS
RET_V_DIM = 2 * D_MODEL // RET_HEADS
ROPE_BASE = 10000.0
EPS = 1e-6
GN_EPS = 1e-5

F32 = jnp.float32
BF16 = jnp.bfloat16

V7X_VMEM_LIMIT_BYTES = 60000 * 1024
LANES = 128


def _params(vmem_bytes):
    assert vmem_bytes <= V7X_VMEM_LIMIT_BYTES, vmem_bytes
    return pltpu.CompilerParams(
        dimension_semantics=("arbitrary", "arbitrary"),
        vmem_limit_bytes=V7X_VMEM_LIMIT_BYTES,
    )


NORM_ROWS = 16
NORM_UNROLL = 4
ROW_CHUNK = 128


def _rms(x, g):
    ms = jnp.mean(x * x, axis=-1, keepdims=True)
    return (x * lax.rsqrt(ms + EPS)) * g


def _rms_rows(src_ref, g_ref, dst_ref):
    g = g_ref[...]
    rows_per_step = NORM_ROWS * NORM_UNROLL
    assert src_ref.shape[0] % rows_per_step == 0

    def step(r, carry):
        base = pl.multiple_of(r * rows_per_step, rows_per_step)
        slabs = [pl.ds(base + u * NORM_ROWS, NORM_ROWS) for u in range(NORM_UNROLL)]
        xs = [src_ref[rows, :] for rows in slabs]
        for rows, x in zip(slabs, xs):
            dst_ref[rows, :] = _rms(x, g).astype(dst_ref.dtype)
        return carry

    lax.fori_loop(0, src_ref.shape[0] // rows_per_step, step, 0)


def _chunk_rows(c):
    return pl.ds(c * ROW_CHUNK, ROW_CHUNK)


def _tile_row0(tm):
    return pl.multiple_of(pl.program_id(0) * tm, tm)


def _stream_rms(h_hbm, g_ref, xn_ref, stage_ref, sem):
    tm = xn_ref.shape[0]
    chunk = stage_ref.shape[1]
    n_c = tm // chunk
    row0 = _tile_row0(tm)

    def copy(c):
        src = h_hbm.at[pl.ds(row0 + c * chunk, chunk), :]
        return pltpu.make_async_copy(src, stage_ref.at[c % 2], sem.at[c % 2])

    copy(0).start()
    for c in range(n_c):
        if c + 1 < n_c:
            copy(c + 1).start()
        copy(c).wait()
        _rms_rows(stage_ref.at[c % 2], g_ref, xn_ref.at[pl.ds(c * chunk, chunk), :])


def _stream_scratch(tm, d, chunk=ROW_CHUNK):
    assert tm % chunk == 0 and chunk % (NORM_ROWS * NORM_UNROLL) == 0
    return [pltpu.VMEM((tm, d), BF16), pltpu.VMEM((2, chunk, d), F32),
            pltpu.SemaphoreType.DMA((2,))]


def _silu(x):
    return x * jax.nn.sigmoid(x)


SIDE_BLOCKS = 256
BF16_SUBLANES = 16


def _side_cast_specs(weights, n_steps, step_of):
    n_blk = 1 << (min(n_steps, SIDE_BLOCKS).bit_length() - 1)
    specs, vmem = [], 0
    for w in weights:
        rows = w.shape[0] // n_blk
        assert w.shape[0] % n_blk == 0 and rows % BF16_SUBLANES == 0
        vmem += 2 * rows * w.shape[1] * (4 + 2)
        specs.append(pl.BlockSpec(
            (rows, w.shape[1]), lambda *idx: (jnp.minimum(step_of(*idx), n_blk - 1), 0)))
    return specs, vmem


def _side_cast(side_in, side_out):
    for src, dst in zip(side_in, side_out):
        dst[...] = src[...].astype(BF16)


OUT_COL_CHUNKS = 8


def _ffn_body(h_hbm, g_ref, wg_ref, wu_ref, wo_ref, fg_ref, *rest, n_j, final, n_side):
    side_in = rest[:n_side]
    out_hbm = rest[n_side]
    side_out = rest[n_side + 1:2 * n_side + 1]
    acc_ref, xn_ref, sem = rest[2 * n_side + 1:]
    j = pl.program_id(1)
    tm, d = acc_ref.shape
    n_c = tm // ROW_CHUNK
    row0 = _tile_row0(tm)

    def row_copy(c, *, load):
        hbm_rows = pl.ds(row0 + c * ROW_CHUNK, ROW_CHUNK)
        vm = acc_ref.at[_chunk_rows(c), :]
        if load:
            return pltpu.make_async_copy(h_hbm.at[hbm_rows, :], vm, sem.at[c])
        return pltpu.make_async_copy(vm, out_hbm.at[hbm_rows, :], sem.at[c])

    def col_copy(c):
        cols = pl.ds(c * (d // OUT_COL_CHUNKS), d // OUT_COL_CHUNKS)
        return pltpu.make_async_copy(acc_ref.at[:, cols], out_hbm.at[pl.ds(row0, tm), cols],
                                     sem.at[c])

    @pl.when(j == 0)
    def _():
        for c in range(n_c):
            row_copy(c, load=True).start()
        for c in range(n_c):
            row_copy(c, load=True).wait()
            _rms_rows(acc_ref.at[_chunk_rows(c), :], g_ref, xn_ref.at[_chunk_rows(c), :])

    def step(last):
        _side_cast(side_in, side_out)
        xn = xn_ref[...]
        gate = jnp.dot(xn, wg_ref[...].astype(BF16), preferred_element_type=F32)
        up = jnp.dot(xn, wu_ref[...].astype(BF16), preferred_element_type=F32)
        act = (0.5 * _silu(gate) * up).astype(BF16)
        wo = wo_ref[...].astype(BF16)
        if not last:
            acc_ref[...] += jnp.dot(act, wo, preferred_element_type=F32)
        elif final:
            acc_ref[...] += jnp.dot(act, wo, preferred_element_type=F32)
            for c in range(n_c):
                rows = acc_ref.at[_chunk_rows(c), :]
                _rms_rows(rows, fg_ref, rows)
                row_copy(c, load=False).start()
            for c in range(n_c):
                row_copy(c, load=False).wait()
        else:
            dc = d // OUT_COL_CHUNKS
            for c in range(OUT_COL_CHUNKS):
                cols = slice(c * dc, (c + 1) * dc)
                acc_ref[:, cols] += jnp.dot(act, wo[:, cols], preferred_element_type=F32)
                col_copy(c).start()
            for c in range(OUT_COL_CHUNKS):
                col_copy(c).wait()

    @pl.when(j < n_j - 1)
    def _():
        step(False)

    @pl.when(j == n_j - 1)
    def _():
        step(True)


def _ffn(h, g, w_in, w_out, final_g=None, side_weights=(), *, tm=1024, tf=256):
    m, d = h.shape
    d_ff = w_out.shape[0]
    n_j = d_ff // tf
    assert d_ff % tf == 0 and m % tm == 0 and tm % ROW_CHUNK == 0
    assert d % (OUT_COL_CHUNKS * LANES) == 0 and OUT_COL_CHUNKS <= tm // ROW_CHUNK
    final = final_g is not None
    fg = final_g if final else g
    side_specs, side_vmem = _side_cast_specs(side_weights, (m // tm) * n_j,
                                             lambda i, j: i * n_j + j)
    vmem = tm * d * 4 + tm * d * 2 + 2 * 3 * d * tf * 4 + side_vmem

    return pl.pallas_call(
        functools.partial(_ffn_body, n_j=n_j, final=final, n_side=len(side_weights)),
        grid=(m // tm, n_j),
        in_specs=[
            pl.BlockSpec(memory_space=pl.ANY),
            pl.BlockSpec((1, d), lambda i, j: (0, 0)),
            pl.BlockSpec((d, tf), lambda i, j: (0, j)),
            pl.BlockSpec((d, tf), lambda i, j: (0, j + n_j)),
            pl.BlockSpec((tf, d), lambda i, j: (j, 0)),
            pl.BlockSpec((1, d), lambda i, j: (0, 0)),
        ] + side_specs,
        out_specs=[pl.BlockSpec(memory_space=pl.ANY)] + side_specs,
        out_shape=[jax.ShapeDtypeStruct((m, d), F32)]
        + [jax.ShapeDtypeStruct(w.shape, BF16) for w in side_weights],
        scratch_shapes=[pltpu.VMEM((tm, d), F32), pltpu.VMEM((tm, d), BF16),
                        pltpu.SemaphoreType.DMA((tm // ROW_CHUNK,))],
        compiler_params=_params(vmem),
        name="ffn",
    )(h, g.reshape(1, d), w_in, w_in, w_out, fg.reshape(1, d), *side_weights)


def _conv_in_body(h_hbm, g_ref, wb_ref, wc_ref, wh_ref, cw_ref, *rest, tiles_per_seq, n_side):
    side_in = rest[:n_side]
    y_ref = rest[n_side]
    side_out = rest[n_side + 1:2 * n_side + 1]
    xn_ref, stage_ref, sem, carry_ref = rest[2 * n_side + 1:]
    i = pl.program_id(0)
    j = pl.program_id(1)
    _side_cast(side_in, side_out)

    @pl.when(j == 0)
    def _():
        _stream_rms(h_hbm, g_ref, xn_ref, stage_ref, sem)

    xn = xn_ref[...]
    b = jnp.dot(xn, wb_ref[...], preferred_element_type=F32)
    c = jnp.dot(xn, wc_ref[...], preferred_element_type=F32)
    hh = jnp.dot(xn, wh_ref[...], preferred_element_type=F32)
    u = c * hh
    tm = u.shape[0]

    @pl.when(i % tiles_per_seq == 0)
    def _():
        carry_ref[j] = jnp.zeros(carry_ref.shape[1:], F32)

    prev = carry_ref[j]
    carry_ref[j] = u[tm - 8:, :]
    row = lax.broadcasted_iota(jnp.int32, u.shape, 0)
    u1 = jnp.where(row < 1, prev[7:8, :], pltpu.roll(u, 1, 0))
    u2 = jnp.where(row < 2, jnp.where(row < 1, prev[6:7, :], prev[7:8, :]), pltpu.roll(u, 2, 0))
    cw = cw_ref[...]
    conv = cw[0:1, :] * u2 + cw[1:2, :] * u1 + cw[2:3, :] * u
    y_ref[...] = (b * conv).astype(BF16)


def _conv_in(h, g, w_in, conv_w, side_weights=(), *, seq, tm=1024, tc=512):
    m, d = h.shape
    n_j = d // tc
    assert m % tm == 0 and seq % tm == 0 and d % tc == 0
    side_specs, side_vmem = _side_cast_specs(side_weights, (m // tm) * n_j,
                                             lambda i, j: i * n_j + j)
    vmem = (2 * ROW_CHUNK * d * 4 + tm * d * 2 + 2 * 3 * d * tc * 2 + 2 * tm * tc * 2
            + n_j * 8 * tc * 4 + 6 * tm * tc * 4 + side_vmem)
    return pl.pallas_call(
        functools.partial(_conv_in_body, tiles_per_seq=seq // tm, n_side=len(side_weights)),
        grid=(m // tm, n_j),
        in_specs=[
            pl.BlockSpec(memory_space=pl.ANY),
            pl.BlockSpec((1, d), lambda i, j: (0, 0)),
            pl.BlockSpec((d, tc), lambda i, j: (0, j)),
            pl.BlockSpec((d, tc), lambda i, j: (0, j + n_j)),
            pl.BlockSpec((d, tc), lambda i, j: (0, j + 2 * n_j)),
            pl.BlockSpec((CONV_WIDTH, tc), lambda i, j: (0, j)),
        ] + side_specs,
        out_specs=[pl.BlockSpec((tm, tc), lambda i, j: (i, j))] + side_specs,
        out_shape=[jax.ShapeDtypeStruct((m, d), BF16)]
        + [jax.ShapeDtypeStruct(w.shape, BF16) for w in side_weights],
        scratch_shapes=_stream_scratch(tm, d) + [pltpu.VMEM((n_j, 8, tc), F32)],
        compiler_params=_params(vmem),
        name="conv_in",
    )(h, g.reshape(1, d), w_in, w_in, w_in, conv_w, *side_weights)


PROJ_ROW_CHUNK = 64


def _proj_body(h_hbm, g_ref, w_ref, cos_ref, sin_ref, out_ref, xn_ref, stage_ref, sem,
               *, n_q_tiles):
    j = pl.program_id(1)

    @pl.when(j == 0)
    def _():
        _stream_rms(h_hbm, g_ref, xn_ref, stage_ref, sem)

    def project():
        return jnp.dot(xn_ref[...], w_ref[...].astype(BF16), preferred_element_type=F32)

    @pl.when(j < 2 * n_q_tiles)
    def _():
        t = project()
        cos = cos_ref[...]
        sin = sin_ref[...]
        scale = jnp.where(j < n_q_tiles, RET_QK_DIM ** -0.5, 1.0)
        half = RET_QK_DIM // 2
        for hd in range(t.shape[1] // RET_QK_DIM):
            lo = hd * RET_QK_DIM
            t1 = t[:, lo:lo + half]
            t2 = t[:, lo + half:lo + 2 * half]
            out_ref[:, lo:lo + half] = ((t1 * cos - t2 * sin) * scale).astype(BF16)
            out_ref[:, lo + half:lo + 2 * half] = ((t1 * sin + t2 * cos) * scale).astype(BF16)

    @pl.when(j >= 2 * n_q_tiles)
    def _():
        out_ref[...] = project().astype(BF16)


def _proj(h, g, w, cos, sin, *, seq, tm=1024, tn=1024):
    m, d = h.shape
    n = w.shape[1]
    assert m % tm == 0 and n % tn == 0 and d % tn == 0 and seq % tm == 0
    half = RET_QK_DIM // 2
    assert half == LANES and tn % RET_QK_DIM == 0
    tiles_per_seq = seq // tm
    vmem = (2 * PROJ_ROW_CHUNK * d * 4 + tm * d * 2 + 2 * d * tn * 4 + 2 * tm * tn * 2
            + 4 * tm * half * 4)
    table_spec = pl.BlockSpec((tm, half), lambda i, j: (i % tiles_per_seq, 0))
    return pl.pallas_call(
        functools.partial(_proj_body, n_q_tiles=d // tn),
        grid=(m // tm, n // tn),
        in_specs=[
            pl.BlockSpec(memory_space=pl.ANY),
            pl.BlockSpec((1, d), lambda i, j: (0, 0)),
            pl.BlockSpec((d, tn), lambda i, j: (0, j)),
            table_spec,
            table_spec,
        ],
        out_specs=pl.BlockSpec((tm, tn), lambda i, j: (i, j)),
        out_shape=jax.ShapeDtypeStruct((m, n), BF16),
        scratch_shapes=_stream_scratch(tm, d, PROJ_ROW_CHUNK),
        compiler_params=_params(vmem),
        name="ret_proj",
    )(h, g.reshape(1, d), w, cos, sin)


def _ret_body(lg_ref, q_ref, k_ref, v_ref, gate_ref, gn_ref, *rest, t_blk, heads, n_side):
    side_in = rest[:n_side]
    out_ref = rest[n_side]
    side_out = rest[n_side + 1:2 * n_side + 1]
    state_ref, dmat_ref, qdec_ref, kdec_ref = rest[2 * n_side + 1:]
    hg = pl.program_id(1)
    t = pl.program_id(2)
    _side_cast(side_in, side_out)

    @pl.when(t == 0)
    def _():
        state_ref[...] = jnp.zeros_like(state_ref)
        row = lax.broadcasted_iota(jnp.int32, (t_blk, t_blk), 0)
        col = lax.broadcasted_iota(jnp.int32, (t_blk, t_blk), 1)
        shift = CHUNK.bit_length() - 1
        rc = lax.shift_right_logical(row, shift)
        cc = lax.shift_right_logical(col, shift)
        diff = row - col
        visible = cc <= rc
        dist = jnp.where(visible, jnp.where(rc == cc, jnp.abs(diff), diff), 0).astype(F32)
        idx = lax.broadcasted_iota(jnp.int32, (t_blk, RET_QK_DIM), 0).astype(F32)
        for hh in range(heads):
            lg = lg_ref[hg * heads + hh]
            dmat_ref[hh] = jnp.where(visible, jnp.exp(lg * dist), 0.0)
            qdec_ref[hh] = jnp.exp(lg * (idx + 1.0))
            kdec_ref[hh] = jnp.exp(lg * (t_blk - 1.0 - idx))

    for hh in range(heads):
        qk_cols = slice(hh * RET_QK_DIM, (hh + 1) * RET_QK_DIM)
        v_cols = slice(hh * RET_V_DIM, (hh + 1) * RET_V_DIM)
        q = q_ref[:, qk_cols]
        k = k_ref[:, qk_cols]
        v = v_ref[:, v_cols]
        s = lax.dot_general(q, k, (((1,), (1,)), ((), ())), preferred_element_type=F32)
        p = (s * dmat_ref[hh]).astype(BF16)
        o = jnp.dot(p, v, preferred_element_type=F32)

        qx = (q.astype(F32) * qdec_ref[hh]).astype(BF16)
        kx = (k.astype(F32) * kdec_ref[hh]).astype(BF16)
        st = state_ref[hh]
        o = o + jnp.dot(qx, st.astype(BF16), preferred_element_type=F32)
        blk_dec = qdec_ref[hh, t_blk - 1:t_blk, 0:1]
        state_ref[hh] = blk_dec * st + lax.dot_general(
            kx, v, (((0,), (0,)), ((), ())), preferred_element_type=F32)

        mu = jnp.mean(o, axis=-1, keepdims=True)
        dlt = o - mu
        var = jnp.mean(dlt * dlt, axis=-1, keepdims=True)
        y = (dlt * lax.rsqrt(var + GN_EPS)) * gn_ref[:, v_cols]
        out_ref[:, v_cols] = (_silu(gate_ref[:, v_cols].astype(F32)) * y).astype(BF16)


def _retention(qkvg, gn_g, log_gamma, side_weights=(), *, bsz, seq, t_blk=256, heads=8):
    m = qkvg.shape[0]
    nt = seq // t_blk
    n_hg = RET_HEADS // heads
    assert seq % t_blk == 0 and t_blk % CHUNK == 0 and RET_HEADS % heads == 0
    qk_w = heads * RET_QK_DIM
    v_w = heads * RET_V_DIM
    side_specs, _ = _side_cast_specs(side_weights, bsz * n_hg * nt,
                                     lambda b, h, t: (b * n_hg + h) * nt + t)
    return pl.pallas_call(
        functools.partial(_ret_body, t_blk=t_blk, heads=heads, n_side=len(side_weights)),
        grid=(bsz, n_hg, nt),
        in_specs=[
            pl.BlockSpec(memory_space=pltpu.SMEM),
            pl.BlockSpec((t_blk, qk_w), lambda b, h, t: (b * nt + t, h)),
            pl.BlockSpec((t_blk, qk_w), lambda b, h, t: (b * nt + t, h + n_hg)),
            pl.BlockSpec((t_blk, v_w), lambda b, h, t: (b * nt + t, h + n_hg)),
            pl.BlockSpec((t_blk, v_w), lambda b, h, t: (b * nt + t, h + 2 * n_hg)),
            pl.BlockSpec((1, v_w), lambda b, h, t: (0, h)),
        ] + side_specs,
        out_specs=[pl.BlockSpec((t_blk, v_w), lambda b, h, t: (b * nt + t, h))] + side_specs,
        out_shape=[jax.ShapeDtypeStruct((m, RET_HEADS * RET_V_DIM), BF16)]
        + [jax.ShapeDtypeStruct(w.shape, BF16) for w in side_weights],
        scratch_shapes=[pltpu.VMEM((heads, RET_QK_DIM, RET_V_DIM), F32),
                        pltpu.VMEM((heads, t_blk, t_blk), F32),
                        pltpu.VMEM((heads, t_blk, RET_QK_DIM), F32),
                        pltpu.VMEM((heads, t_blk, RET_QK_DIM), F32)],
        compiler_params=pltpu.CompilerParams(
            dimension_semantics=("arbitrary", "arbitrary", "arbitrary")),
        name="retention",
    )(log_gamma, qkvg, qkvg, qkvg, qkvg, gn_g.reshape(1, -1), *side_weights)


def _out_proj_body(y_ref, w_ref, h_ref, out_ref):
    out_ref[...] = h_ref[...] + jnp.dot(y_ref[...], w_ref[...], preferred_element_type=F32)


def _out_proj(y, w, h, *, tm, tn, name):
    m, kdim = y.shape
    d = w.shape[1]
    assert m % tm == 0 and d % tn == 0
    assert y.dtype == BF16 and w.dtype == BF16
    vmem = 2 * tm * kdim * 2 + 2 * kdim * tn * 2 + 4 * tm * tn * 4
    return pl.pallas_call(
        _out_proj_body,
        grid=(m // tm, d // tn),
        in_specs=[
            pl.BlockSpec((tm, kdim), lambda i, j: (i, 0)),
            pl.BlockSpec((kdim, tn), lambda i, j: (0, j)),
            pl.BlockSpec((tm, tn), lambda i, j: (i, j)),
        ],
        out_specs=pl.BlockSpec((tm, tn), lambda i, j: (i, j)),
        out_shape=jax.ShapeDtypeStruct((m, d), F32),
        compiler_params=_params(vmem),
        name=name,
    )(y, w, h)


def kernel(x, l0_norm_ffn1, l0_ffn1_w_in, l0_ffn1_w_out, l0_norm_mix, l0_conv_w_in, l0_conv_w, l0_conv_w_out, l0_norm_ffn2, l0_ffn2_w_in, l0_ffn2_w_out, l1_norm_ffn1, l1_ffn1_w_in, l1_ffn1_w_out, l1_norm_mix, l1_ret_w_in, l1_ret_gn, l1_ret_w_out, l1_norm_ffn2, l1_ffn2_w_in, l1_ffn2_w_out, final_norm):
    bsz, seq, d = x.shape
    h = x.reshape(bsz * seq, d)

    h, conv_w_in = _ffn(h, l0_norm_ffn1, l0_ffn1_w_in, l0_ffn1_w_out,
                        side_weights=(l0_conv_w_in,))
    y, conv_w_out = _conv_in(h, l0_norm_mix, conv_w_in, l0_conv_w,
                             side_weights=(l0_conv_w_out,), seq=seq)
    h = _out_proj(y, conv_w_out, h, tm=1024, tn=512, name="conv_out_proj")
    h, = _ffn(h, l0_norm_ffn2, l0_ffn2_w_in, l0_ffn2_w_out)

    h, = _ffn(h, l1_norm_ffn1, l1_ffn1_w_in, l1_ffn1_w_out)

    half = RET_QK_DIM // 2
    inv = ROPE_BASE ** (-jnp.arange(half, dtype=F32) / half)
    ang = jnp.arange(seq, dtype=jnp.int32).astype(F32)[:, None] * inv[None, :]
    cos, sin = jnp.cos(ang), jnp.sin(ang)
    log_gamma = jnp.log1p(-jnp.exp2(-5.0 - jnp.arange(RET_HEADS, dtype=F32)))

    qkvg = _proj(h, l1_norm_mix, l1_ret_w_in, cos, sin, seq=seq)
    y, ret_w_out = _retention(qkvg, l1_ret_gn, log_gamma, side_weights=(l1_ret_w_out,),
                              bsz=bsz, seq=seq)
    h = _out_proj(y, ret_w_out, h, tm=512, tn=512, name="ret_out_proj")

    h, = _ffn(h, l1_norm_ffn2, l1_ffn2_w_in, l1_ffn2_w_out, final_norm)
    return h.reshape(bsz, seq, d)
```

```python
import functools

import jax
import jax.numpy as jnp
from jax import lax
from jax.experimental import pallas as pl
from jax.experimental.pallas import tpu as pltpu

D_MODEL = 4096
CHUNK = 64
CONV_WIDTH = 3
RET_HEADS = 16
RET_QK_DIM = D_MODEL // RET_HEADS
RET_V_DIM = 2 * D_MODEL // RET_HEADS
ROPE_BASE = 10000.0
EPS = 1e-6
GN_EPS = 1e-5

F32 = jnp.float32
BF16 = jnp.bfloat16

V7X_VMEM_LIMIT_BYTES = 60000 * 1024
LANES = 128


def _params(vmem_bytes):
    assert vmem_bytes <= V7X_VMEM_LIMIT_BYTES, vmem_bytes
    return pltpu.CompilerParams(
        dimension_semantics=("arbitrary", "arbitrary"),
        vmem_limit_bytes=V7X_VMEM_LIMIT_BYTES,
    )


NORM_ROWS = 16
NORM_UNROLL = 4
ROW_CHUNK = 128


def _rms(x, g):
    ms = jnp.mean(x * x, axis=-1, keepdims=True)
    return (x * lax.rsqrt(ms + EPS)) * g


def _rms_rows(src_ref, g_ref, dst_ref):
    g = g_ref[...]
    rows_per_step = NORM_ROWS * NORM_UNROLL
    assert src_ref.shape[0] % rows_per_step == 0

    def step(r, carry):
        base = pl.multiple_of(r * rows_per_step, rows_per_step)
        slabs = [pl.ds(base + u * NORM_ROWS, NORM_ROWS) for u in range(NORM_UNROLL)]
        xs = [src_ref[rows, :] for rows in slabs]
        for rows, x in zip(slabs, xs):
            dst_ref[rows, :] = _rms(x, g).astype(dst_ref.dtype)
        return carry

    lax.fori_loop(0, src_ref.shape[0] // rows_per_step, step, 0)


def _chunk_rows(c):
    return pl.ds(c * ROW_CHUNK, ROW_CHUNK)


def _tile_row0(tm):
    return pl.multiple_of(pl.program_id(0) * tm, tm)


def _stream_rms(h_hbm, g_ref, xn_ref, stage_ref, sem):
    tm = xn_ref.shape[0]
    chunk = stage_ref.shape[1]
    n_c = tm // chunk
    row0 = _tile_row0(tm)

    def copy(c):
        src = h_hbm.at[pl.ds(row0 + c * chunk, chunk), :]
        return pltpu.make_async_copy(src, stage_ref.at[c % 2], sem.at[c % 2])

    copy(0).start()
    for c in range(n_c):
        if c + 1 < n_c:
            copy(c + 1).start()
        copy(c).wait()
        _rms_rows(stage_ref.at[c % 2], g_ref, xn_ref.at[pl.ds(c * chunk, chunk), :])


def _stream_scratch(tm, d, chunk=ROW_CHUNK):
    assert tm % chunk == 0 and chunk % (NORM_ROWS * NORM_UNROLL) == 0
    return [pltpu.VMEM((tm, d), BF16), pltpu.VMEM((2, chunk, d), F32),
            pltpu.SemaphoreType.DMA((2,))]


def _silu(x):
    return x * jax.nn.sigmoid(x)


SIDE_BLOCKS = 256
BF16_SUBLANES = 16


def _side_cast_specs(weights, n_steps, step_of):
    n_blk = 1 << (min(n_steps, SIDE_BLOCKS).bit_length() - 1)
    specs, vmem = [], 0
    for w in weights:
        rows = w.shape[0] // n_blk
        assert w.shape[0] % n_blk == 0 and rows % BF16_SUBLANES == 0
        vmem += 2 * rows * w.shape[1] * (4 + 2)
        specs.append(pl.BlockSpec(
            (rows, w.shape[1]), lambda *idx: (jnp.minimum(step_of(*idx), n_blk - 1), 0)))
    return specs, vmem


def _side_cast(side_in, side_out):
    for src, dst in zip(side_in, side_out):
        dst[...] = src[...].astype(BF16)


OUT_COL_CHUNKS = 8


def _ffn_body(h_hbm, g_ref, wg_ref, wu_ref, wo_ref, fg_ref, *rest, n_j, final, n_side):
    side_in = rest[:n_side]
    out_hbm = rest[n_side]
    side_out = rest[n_side + 1:2 * n_side + 1]
    acc_ref, xn_ref, sem = rest[2 * n_side + 1:]
    j = pl.program_id(1)
    tm, d = acc_ref.shape
    n_c = tm // ROW_CHUNK
    row0 = _tile_row0(tm)

    def row_copy(c, *, load):
        hbm_rows = pl.ds(row0 + c * ROW_CHUNK, ROW_CHUNK)
        vm = acc_ref.at[_chunk_rows(c), :]
        if load:
            return pltpu.make_async_copy(h_hbm.at[hbm_rows, :], vm, sem.at[c])
        return pltpu.make_async_copy(vm, out_hbm.at[hbm_rows, :], sem.at[c])

    def col_copy(c):
        cols = pl.ds(c * (d // OUT_COL_CHUNKS), d // OUT_COL_CHUNKS)
        return pltpu.make_async_copy(acc_ref.at[:, cols], out_hbm.at[pl.ds(row0, tm), cols],
                                     sem.at[c])

    @pl.when(j == 0)
    def _():
        for c in range(n_c):
            row_copy(c, load=True).start()
        for c in range(n_c):
            row_copy(c, load=True).wait()
            _rms_rows(acc_ref.at[_chunk_rows(c), :], g_ref, xn_ref.at[_chunk_rows(c), :])

    def step(last):
        _side_cast(side_in, side_out)
        xn = xn_ref[...]
        gate = jnp.dot(xn, wg_ref[...].astype(BF16), preferred_element_type=F32)
        up = jnp.dot(xn, wu_ref[...].astype(BF16), preferred_element_type=F32)
        act = (0.5 * _silu(gate) * up).astype(BF16)
        wo = wo_ref[...].astype(BF16)
        if not last:
            acc_ref[...] += jnp.dot(act, wo, preferred_element_type=F32)
        elif final:
            acc_ref[...] += jnp.dot(act, wo, preferred_element_type=F32)
            for c in range(n_c):
                rows = acc_ref.at[_chunk_rows(c), :]
                _rms_rows(rows, fg_ref, rows)
                row_copy(c, load=False).start()
            for c in range(n_c):
                row_copy(c, load=False).wait()
        else:
            dc = d // OUT_COL_CHUNKS
            for c in range(OUT_COL_CHUNKS):
                cols = slice(c * dc, (c + 1) * dc)
                acc_ref[:, cols] += jnp.dot(act, wo[:, cols], preferred_element_type=F32)
                col_copy(c).start()
            for c in range(OUT_COL_CHUNKS):
                col_copy(c).wait()

    @pl.when(j < n_j - 1)
    def _():
        step(False)

    @pl.when(j == n_j - 1)
    def _():
        step(True)


def _ffn(h, g, w_in, w_out, final_g=None, side_weights=(), *, tm=1024, tf=256):
    m, d = h.shape
    d_ff = w_out.shape[0]
    n_j = d_ff // tf
    assert d_ff % tf == 0 and m % tm == 0 and tm % ROW_CHUNK == 0
    assert d % (OUT_COL_CHUNKS * LANES) == 0 and OUT_COL_CHUNKS <= tm // ROW_CHUNK
    final = final_g is not None
    fg = final_g if final else g
    side_specs, side_vmem = _side_cast_specs(side_weights, (m // tm) * n_j,
                                             lambda i, j: i * n_j + j)
    vmem = tm * d * 4 + tm * d * 2 + 2 * 3 * d * tf * 4 + side_vmem

    return pl.pallas_call(
        functools.partial(_ffn_body, n_j=n_j, final=final, n_side=len(side_weights)),
        grid=(m // tm, n_j),
        in_specs=[
            pl.BlockSpec(memory_space=pl.ANY),
            pl.BlockSpec((1, d), lambda i, j: (0, 0)),
            pl.BlockSpec((d, tf), lambda i, j: (0, j)),
            pl.BlockSpec((d, tf), lambda i, j: (0, j + n_j)),
            pl.BlockSpec((tf, d), lambda i, j: (j, 0)),
            pl.BlockSpec((1, d), lambda i, j: (0, 0)),
        ] + side_specs,
        out_specs=[pl.BlockSpec(memory_space=pl.ANY)] + side_specs,
        out_shape=[jax.ShapeDtypeStruct((m, d), F32)]
        + [jax.ShapeDtypeStruct(w.shape, BF16) for w in side_weights],
        scratch_shapes=[pltpu.VMEM((tm, d), F32), pltpu.VMEM((tm, d), BF16),
                        pltpu.SemaphoreType.DMA((tm // ROW_CHUNK,))],
        compiler_params=_params(vmem),
        name="ffn",
    )(h, g.reshape(1, d), w_in, w_in, w_out, fg.reshape(1, d), *side_weights)


def _conv_in_body(h_hbm, g_ref, wb_ref, wc_ref, wh_ref, cw_ref, *rest, tiles_per_seq, n_side):
    side_in = rest[:n_side]
    y_ref = rest[n_side]
    side_out = rest[n_side + 1:2 * n_side + 1]
    xn_ref, stage_ref, sem, carry_ref, pend_b_ref, pend_u_ref = rest[2 * n_side + 1:]
    i = pl.program_id(0)
    j = pl.program_id(1)
    n_j = carry_ref.shape[0]
    jp = jnp.maximum(j - 1, 0)
    _side_cast(side_in, side_out)

    @pl.when(j == 0)
    def _():
        _stream_rms(h_hbm, g_ref, xn_ref, stage_ref, sem)

    @pl.when(i % tiles_per_seq == 0)
    def _():
        carry_ref[jp] = jnp.zeros(carry_ref.shape[1:], F32)

    def project():
        xn = xn_ref[...]
        b = jnp.dot(xn, wb_ref[...], preferred_element_type=F32)
        c = jnp.dot(xn, wc_ref[...], preferred_element_type=F32)
        hh = jnp.dot(xn, wh_ref[...], preferred_element_type=F32)
        return b, c * hh

    def gate_pending():
        b = pend_b_ref[...]
        u = pend_u_ref[...]
        tm = u.shape[0]
        prev = carry_ref[jp]
        carry_ref[jp] = u[tm - 8:, :]
        row = lax.broadcasted_iota(jnp.int32, u.shape, 0)
        u1 = jnp.where(row < 1, prev[7:8, :], pltpu.roll(u, 1, 0))
        u2 = jnp.where(row < 2, jnp.where(row < 1, prev[6:7, :], prev[7:8, :]),
                       pltpu.roll(u, 2, 0))
        cw = cw_ref[...]
        conv = cw[0:1, :] * u2 + cw[1:2, :] * u1 + cw[2:3, :] * u
        y_ref[...] = (b * conv).astype(BF16)

    @pl.when(j == 0)
    def _():
        pend_b_ref[...], pend_u_ref[...] = project()

    @pl.when(jnp.logical_and(j > 0, j < n_j))
    def _():
        gate_pending()
        pend_b_ref[...], pend_u_ref[...] = project()

    @pl.when(j == n_j)
    def _():
        gate_pending()


def _conv_in(h, g, w_in, conv_w, side_weights=(), *, seq, tm=1024, tc=512):
    m, d = h.shape
    n_j = d // tc
    assert m % tm == 0 and seq % tm == 0 and d % tc == 0
    side_specs, side_vmem = _side_cast_specs(side_weights, (m // tm) * (n_j + 1),
                                             lambda i, j: i * (n_j + 1) + j)
    vmem = (2 * ROW_CHUNK * d * 4 + tm * d * 2 + 2 * 3 * d * tc * 2 + 2 * tm * tc * 2
            + n_j * 8 * tc * 4 + 8 * tm * tc * 4 + side_vmem)

    def proj_tile(j):
        return jnp.minimum(j, n_j - 1)

    def gate_tile(j):
        return jnp.maximum(j - 1, 0)

    return pl.pallas_call(
        functools.partial(_conv_in_body, tiles_per_seq=seq // tm, n_side=len(side_weights)),
        grid=(m // tm, n_j + 1),
        in_specs=[
            pl.BlockSpec(memory_space=pl.ANY),
            pl.BlockSpec((1, d), lambda i, j: (0, 0)),
            pl.BlockSpec((d, tc), lambda i, j: (0, proj_tile(j))),
            pl.BlockSpec((d, tc), lambda i, j: (0, proj_tile(j) + n_j)),
            pl.BlockSpec((d, tc), lambda i, j: (0, proj_tile(j) + 2 * n_j)),
            pl.BlockSpec((CONV_WIDTH, tc), lambda i, j: (0, gate_tile(j))),
        ] + side_specs,
        out_specs=[pl.BlockSpec((tm, tc), lambda i, j: (i, gate_tile(j)))] + side_specs,
        out_shape=[jax.ShapeDtypeStruct((m, d), BF16)]
        + [jax.ShapeDtypeStruct(w.shape, BF16) for w in side_weights],
        scratch_shapes=_stream_scratch(tm, d) + [pltpu.VMEM((n_j, 8, tc), F32),
                                                 pltpu.VMEM((tm, tc), F32),
                                                 pltpu.VMEM((tm, tc), F32)],
        compiler_params=_params(vmem),
        name="conv_in",
    )(h, g.reshape(1, d), w_in, w_in, w_in, conv_w, *side_weights)


PROJ_ROW_CHUNK = 64


def _proj_body(h_hbm, g_ref, w_ref, cos_ref, sin_ref, out_ref, xn_ref, stage_ref, sem,
               *, n_q_tiles):
    j = pl.program_id(1)

    @pl.when(j == 0)
    def _():
        _stream_rms(h_hbm, g_ref, xn_ref, stage_ref, sem)

    def project():
        return jnp.dot(xn_ref[...], w_ref[...].astype(BF16), preferred_element_type=F32)

    @pl.when(j < 2 * n_q_tiles)
    def _():
        t = project()
        cos = cos_ref[...]
        sin = sin_ref[...]
        scale = jnp.where(j < n_q_tiles, RET_QK_DIM ** -0.5, 1.0)
        half = RET_QK_DIM // 2
        for hd in range(t.shape[1] // RET_QK_DIM):
            lo = hd * RET_QK_DIM
            t1 = t[:, lo:lo + half]
            t2 = t[:, lo + half:lo + 2 * half]
            out_ref[:, lo:lo + half] = ((t1 * cos - t2 * sin) * scale).astype(BF16)
            out_ref[:, lo + half:lo + 2 * half] = ((t1 * sin + t2 * cos) * scale).astype(BF16)

    @pl.when(j >= 2 * n_q_tiles)
    def _():
        out_ref[...] = project().astype(BF16)


def _proj(h, g, w, cos, sin, *, seq, tm=1024, tn=1024):
    m, d = h.shape
    n = w.shape[1]
    assert m % tm == 0 and n % tn == 0 and d % tn == 0 and seq % tm == 0
    half = RET_QK_DIM // 2
    assert half == LANES and tn % RET_QK_DIM == 0
    tiles_per_seq = seq // tm
    vmem = (2 * PROJ_ROW_CHUNK * d * 4 + tm * d * 2 + 2 * d * tn * 4 + 2 * tm * tn * 2
            + 4 * tm * half * 4)
    table_spec = pl.BlockSpec((tm, half), lambda i, j: (i % tiles_per_seq, 0))
    return pl.pallas_call(
        functools.partial(_proj_body, n_q_tiles=d // tn),
        grid=(m // tm, n // tn),
        in_specs=[
            pl.BlockSpec(memory_space=pl.ANY),
            pl.BlockSpec((1, d), lambda i, j: (0, 0)),
            pl.BlockSpec((d, tn), lambda i, j: (0, j)),
            table_spec,
            table_spec,
        ],
        out_specs=pl.BlockSpec((tm, tn), lambda i, j: (i, j)),
        out_shape=jax.ShapeDtypeStruct((m, n), BF16),
        scratch_shapes=_stream_scratch(tm, d, PROJ_ROW_CHUNK),
        compiler_params=_params(vmem),
        name="ret_proj",
    )(h, g.reshape(1, d), w, cos, sin)


def _ret_body(lg_ref, q_ref, k_ref, v_ref, gate_ref, gn_ref, *rest, t_blk, heads, n_side):
    side_in = rest[:n_side]
    out_ref = rest[n_side]
    side_out = rest[n_side + 1:2 * n_side + 1]
    state_ref, dmat_ref, qdec_ref, kdec_ref = rest[2 * n_side + 1:]
    hg = pl.program_id(1)
    t = pl.program_id(2)
    _side_cast(side_in, side_out)

    @pl.when(t == 0)
    def _():
        state_ref[...] = jnp.zeros_like(state_ref)
        row = lax.broadcasted_iota(jnp.int32, (t_blk, t_blk), 0)
        col = lax.broadcasted_iota(jnp.int32, (t_blk, t_blk), 1)
        shift = CHUNK.bit_length() - 1
        rc = lax.shift_right_logical(row, shift)
        cc = lax.shift_right_logical(col, shift)
        diff = row - col
        visible = cc <= rc
        dist = jnp.where(visible, jnp.where(rc == cc, jnp.abs(diff), diff), 0).astype(F32)
        idx = lax.broadcasted_iota(jnp.int32, (t_blk, RET_QK_DIM), 0).astype(F32)
        for hh in range(heads):
            lg = lg_ref[hg * heads + hh]
            dmat_ref[hh] = jnp.where(visible, jnp.exp(lg * dist), 0.0)
            qdec_ref[hh] = jnp.exp(lg * (idx + 1.0))
            kdec_ref[hh] = jnp.exp(lg * (t_blk - 1.0 - idx))

    for hh in range(heads):
        qk_cols = slice(hh * RET_QK_DIM, (hh + 1) * RET_QK_DIM)
        v_cols = slice(hh * RET_V_DIM, (hh + 1) * RET_V_DIM)
        q = q_ref[:, qk_cols]
        k = k_ref[:, qk_cols]
        v = v_ref[:, v_cols]
        s = lax.dot_general(q, k, (((1,), (1,)), ((), ())), preferred_element_type=F32)
        p = (s * dmat_ref[hh]).astype(BF16)
        o = jnp.dot(p, v, preferred_element_type=F32)

        qx = (q.astype(F32) * qdec_ref[hh]).astype(BF16)
        kx = (k.astype(F32) * kdec_ref[hh]).astype(BF16)
        st = state_ref[hh]
        o = o + jnp.dot(qx, st.astype(BF16), preferred_element_type=F32)
        blk_dec = qdec_ref[hh, t_blk - 1:t_blk, 0:1]
        state_ref[hh] = blk_dec * st + lax.dot_general(
            kx, v, (((0,), (0,)), ((), ())), preferred_element_type=F32)

        mu = jnp.mean(o, axis=-1, keepdims=True)
        dlt = o - mu
        var = jnp.mean(dlt * dlt, axis=-1, keepdims=True)
        y = (dlt * lax.rsqrt(var + GN_EPS)) * gn_ref[:, v_cols]
        out_ref[:, v_cols] = (_silu(gate_ref[:, v_cols].astype(F32)) * y).astype(BF16)


def _retention(qkvg, gn_g, log_gamma, side_weights=(), *, bsz, seq, t_blk=256, heads=8):
    m = qkvg.shape[0]
    nt = seq // t_blk
    n_hg = RET_HEADS // heads
    assert seq % t_blk == 0 and t_blk % CHUNK == 0 and RET_HEADS % heads == 0
    qk_w = heads * RET_QK_DIM
    v_w = heads * RET_V_DIM
    side_specs, _ = _side_cast_specs(side_weights, bsz * n_hg * nt,
                                     lambda b, h, t: (b * n_hg + h) * nt + t)
    return pl.pallas_call(
        functools.partial(_ret_body, t_blk=t_blk, heads=heads, n_side=len(side_weights)),
        grid=(bsz, n_hg, nt),
        in_specs=[
            pl.BlockSpec(memory_space=pltpu.SMEM),
            pl.BlockSpec((t_blk, qk_w), lambda b, h, t: (b * nt + t, h)),
            pl.BlockSpec((t_blk, qk_w), lambda b, h, t: (b * nt + t, h + n_hg)),
            pl.BlockSpec((t_blk, v_w), lambda b, h, t: (b * nt + t, h + n_hg)),
            pl.BlockSpec((t_blk, v_w), lambda b, h, t: (b * nt + t, h + 2 * n_hg)),
            pl.BlockSpec((1, v_w), lambda b, h, t: (0, h)),
        ] + side_specs,
        out_specs=[pl.BlockSpec((t_blk, v_w), lambda b, h, t: (b * nt + t, h))] + side_specs,
        out_shape=[jax.ShapeDtypeStruct((m, RET_HEADS * RET_V_DIM), BF16)]
        + [jax.ShapeDtypeStruct(w.shape, BF16) for w in side_weights],
        scratch_shapes=[pltpu.VMEM((heads, RET_QK_DIM, RET_V_DIM), F32),
                        pltpu.VMEM((heads, t_blk, t_blk), F32),
                        pltpu.VMEM((heads, t_blk, RET_QK_DIM), F32),
                        pltpu.VMEM((heads, t_blk, RET_QK_DIM), F32)],
        compiler_params=pltpu.CompilerParams(
            dimension_semantics=("arbitrary", "arbitrary", "arbitrary")),
        name="retention",
    )(log_gamma, qkvg, qkvg, qkvg, qkvg, gn_g.reshape(1, -1), *side_weights)


def _out_proj_body(y_ref, w_ref, h_ref, out_ref):
    out_ref[...] = h_ref[...] + jnp.dot(y_ref[...], w_ref[...], preferred_element_type=F32)


def _out_proj(y, w, h, *, tm, tn, name):
    m, kdim = y.shape
    d = w.shape[1]
    assert m % tm == 0 and d % tn == 0
    assert y.dtype == BF16 and w.dtype == BF16
    vmem = 2 * tm * kdim * 2 + 2 * kdim * tn * 2 + 4 * tm * tn * 4
    return pl.pallas_call(
        _out_proj_body,
        grid=(m // tm, d // tn),
        in_specs=[
            pl.BlockSpec((tm, kdim), lambda i, j: (i, 0)),
            pl.BlockSpec((kdim, tn), lambda i, j: (0, j)),
            pl.BlockSpec((tm, tn), lambda i, j: (i, j)),
        ],
        out_specs=pl.BlockSpec((tm, tn), lambda i, j: (i, j)),
        out_shape=jax.ShapeDtypeStruct((m, d), F32),
        compiler_params=_params(vmem),
        name=name,
    )(y, w, h)


def kernel(x, l0_norm_ffn1, l0_ffn1_w_in, l0_ffn1_w_out, l0_norm_mix, l0_conv_w_in, l0_conv_w, l0_conv_w_out, l0_norm_ffn2, l0_ffn2_w_in, l0_ffn2_w_out, l1_norm_ffn1, l1_ffn1_w_in, l1_ffn1_w_out, l1_norm_mix, l1_ret_w_in, l1_ret_gn, l1_ret_w_out, l1_norm_ffn2, l1_ffn2_w_in, l1_ffn2_w_out, final_norm):
    bsz, seq, d = x.shape
    h = x.reshape(bsz * seq, d)

    h, conv_w_in = _ffn(h, l0_norm_ffn1, l0_ffn1_w_in, l0_ffn1_w_out,
                        side_weights=(l0_conv_w_in,))
    y, conv_w_out = _conv_in(h, l0_norm_mix, conv_w_in, l0_conv_w,
                             side_weights=(l0_conv_w_out,), seq=seq)
    h = _out_proj(y, conv_w_out, h, tm=1024, tn=512, name="conv_out_proj")
    h, = _ffn(h, l0_norm_ffn2, l0_ffn2_w_in, l0_ffn2_w_out)

    h, = _ffn(h, l1_norm_ffn1, l1_ffn1_w_in, l1_ffn1_w_out)

    half = RET_QK_DIM // 2
    inv = ROPE_BASE ** (-jnp.arange(half, dtype=F32) / half)
    ang = jnp.arange(seq, dtype=jnp.int32).astype(F32)[:, None] * inv[None, :]
    cos, sin = jnp.cos(ang), jnp.sin(ang)
    log_gamma = jnp.log1p(-jnp.exp2(-5.0 - jnp.arange(RET_HEADS, dtype=F32)))

    qkvg = _proj(h, l1_norm_mix, l1_ret_w_in, cos, sin, seq=seq)
    y, ret_w_out = _retention(qkvg, l1_ret_gn, log_gamma, side_weights=(l1_ret_w_out,),
                              bsz=bsz, seq=seq)
    h = _out_proj(y, ret_w_out, h, tm=512, tn=512, name="ret_out_proj")

    h, = _ffn(h, l1_norm_ffn2, l1_ffn2_w_in, l1_ffn2_w_out, final_norm)
    return h.reshape(bsz, seq, d)
```
